```python
import math
import jax, jax.numpy as jnp
from jax import lax
import numpy as np

D_MODEL = 2048
BATCH = 1
SEQ = 8192
DEPTH = 4

W_SSM = 1024
SSM_GROUP = 16
SSM_GROUPS = W_SSM // SSM_GROUP
SSM_STATE = 64
SSM_STEP_MIN = 1e-3
SSM_STEP_MAX = 1e-1
W_RET = 1024
RET_HEADS = 8
RET_HEAD_DIM = W_RET // RET_HEADS
RET_CHUNK = 128
ROPE_BASE = 10000.0
W_RWKV = 1024
RWKV_HEAD_DIM = 64
RWKV_HEADS = W_RWKV // RWKV_HEAD_DIM
RWKV_DECAY_RANK = 64
RWKV_A_RANK = 64
RWKV_GATE_RANK = 128
RWKV_LO = RWKV_DECAY_RANK + RWKV_A_RANK + RWKV_GATE_RANK
RWKV_IN = 3 * W_RWKV + RWKV_LO
N_BRANCH = 3
D_FF = 5632
CONV_WIDTH = 3
OFF_SSM = 0
OFF_RET = OFF_SSM + W_SSM
OFF_RWKV = OFF_RET + 4 * W_RET
OFF_GATE = OFF_RWKV + RWKV_IN
N_IN = OFF_GATE + N_BRANCH * D_MODEL
DEEPNORM_ALPHA = (2.0 * DEPTH) ** 0.25
DEEPNORM_BETA = (8.0 * DEPTH) ** -0.25
LN_EPS = 1e-5
GN_EPS = 1e-5
RWKV_GN_EPS = 64e-5

kernel_name = "hybrid_s5_retnet_rwkv7_deepnorm"


def layer_norm(x, g, b):
    xf = x.astype(jnp.float32)
    mu = xf.mean(-1, keepdims=True)
    var = jnp.mean(jnp.square(xf - mu), -1, keepdims=True)
    return ((xf - mu) * lax.rsqrt(var + LN_EPS) * g + b).astype(x.dtype)


def head_norm(y, g, b, n_heads, eps):
    bsz, t, w = y.shape
    yh = y.astype(jnp.float32).reshape(bsz, t, n_heads, w // n_heads)
    mu = yh.mean(-1, keepdims=True)
    var = jnp.mean(jnp.square(yh - mu), -1, keepdims=True)
    return ((yh - mu) * lax.rsqrt(var + eps)).reshape(bsz, t, w) * g + b


def shift_right(z):
    return jnp.pad(z[:, :-1], ((0, 0), (1, 0), (0, 0)))


def causal_dwconv(h, w):
    kw = w.shape[0]
    t = h.shape[1]
    hp = jnp.pad(h, ((0, 0), (kw - 1, 0), (0, 0)))
    out = hp[:, 0:t] * w[0]
    for j in range(1, kw):
        out = out + hp[:, j:j + t] * w[j]
    return out


def rotary(x, positions):
    half = x.shape[-1] // 2
    inv_freq = ROPE_BASE ** (-jnp.arange(half, dtype=jnp.float32) / half)
    ang = positions.astype(jnp.float32)[..., None] * inv_freq
    cos = jnp.cos(ang)[:, :, None, :]
    sin = jnp.sin(ang)[:, :, None, :]
    x1, x2 = x[..., :half], x[..., half:]
    return jnp.concatenate([x1 * cos - x2 * sin, x2 * cos + x1 * sin], axis=-1)


def s5_branch(u, lam_re, lam_im, log_step, b_re, b_im, c_re, c_im, d_skip, w_glu):
    bsz, t, _ = u.shape
    uf = u.astype(jnp.float32)
    ug = uf.reshape(bsz, t, SSM_GROUPS, SSM_GROUP)
    step = jnp.exp(log_step.astype(jnp.float32))[:, None]
    lr = lam_re.astype(jnp.float32)
    li = lam_im.astype(jnp.float32)
    mag = jnp.exp(lr * step)
    ab_re = mag * jnp.cos(li * step)
    ab_im = mag * jnp.sin(li * step)
    denom = lr * lr + li * li
    f_re = ((ab_re - 1.0) * lr + ab_im * li) / denom
    f_im = (ab_im * lr - (ab_re - 1.0) * li) / denom
    bb_re = f_re[..., None] * b_re - f_im[..., None] * b_im
    bb_im = f_re[..., None] * b_im + f_im[..., None] * b_re
    bu_re = jnp.einsum('btgc,gpc->btgp', ug, bb_re)
    bu_im = jnp.einsum('btgc,gpc->btgp', ug, bb_im)
    a_re = jnp.broadcast_to(ab_re, bu_re.shape)
    a_im = jnp.broadcast_to(ab_im, bu_re.shape)

    def combine(e1, e2):
        a1r, a1i, b1r, b1i = e1
        a2r, a2i, b2r, b2i = e2
        return (a2r * a1r - a2i * a1i,
                a2r * a1i + a2i * a1r,
                a2r * b1r - a2i * b1i + b2r,
                a2r * b1i + a2i * b1r + b2i)

    _, _, s_re, s_im = lax.associative_scan(combine, (a_re, a_im, bu_re, bu_im), axis=1)
    y = (jnp.einsum('btgp,gcp->btgc', s_re, c_re)
         - jnp.einsum('btgp,gcp->btgc', s_im, c_im))
    y = y.reshape(bsz, t, W_SSM) + d_skip * uf
    z = jax.nn.gelu(y).astype(u.dtype)
    h = z @ w_glu
    return h[..., :D_MODEL] * jax.nn.sigmoid(h[..., D_MODEL:])


def retention_branch(q, k, v, g, positions, norm_g, norm_b, w_out):
    bsz, t, _ = q.shape
    c = RET_CHUNK
    n = t // c

    def heads(z):
        return z.astype(jnp.float32).reshape(bsz, t, RET_HEADS, RET_HEAD_DIM)

    qh = rotary(heads(q), positions)
    kh = rotary(heads(k), positions) * (RET_HEAD_DIM ** -0.5)
    vh = heads(v)
    qh = qh.reshape(bsz, n, c, RET_HEADS, RET_HEAD_DIM)
    kh = kh.reshape(bsz, n, c, RET_HEADS, RET_HEAD_DIM)
    vh = vh.reshape(bsz, n, c, RET_HEADS, RET_HEAD_DIM)
    log_gamma = jnp.log1p(-jnp.exp2(-5.0 - jnp.arange(RET_HEADS, dtype=jnp.float32)))
    idx = jnp.arange(c, dtype=jnp.float32)
    rel = idx[:, None] - idx[None, :]
    decay = jnp.where(rel >= 0, jnp.exp(log_gamma[:, None, None] * jnp.maximum(rel, 0.0)), 0.0)
    scores = jnp.einsum('bnihd,bnjhd->bnhij', qh, kh) * decay
    inner = jnp.einsum('bnhij,bnjhe->bnihe', scores, vh)
    k_decay = jnp.exp(log_gamma[None, :] * (c - 1.0 - idx)[:, None])
    kv = jnp.einsum('bnjhd,jh,bnjhe->nbhde', kh, k_decay, vh)
    chunk_decay = jnp.exp(log_gamma * c)[None, :, None, None]

    def step(state, kv_n):
        return chunk_decay * state + kv_n, state

    init = jnp.zeros((bsz, RET_HEADS, RET_HEAD_DIM, RET_HEAD_DIM), jnp.float32)
    _, prev = lax.scan(step, init, kv)
    q_decay = jnp.exp(log_gamma[None, :] * (idx + 1.0)[:, None])
    cross = jnp.einsum('bnihd,nbhde->bnihe', qh, prev) * q_decay[:, :, None]
    o = (inner + cross).reshape(bsz, t, W_RET)
    o = head_norm(o, norm_g, norm_b, RET_HEADS, GN_EPS)
    o = jax.nn.silu(g.astype(jnp.float32)) * o
    return o.astype(q.dtype) @ w_out


def rwkv7_branch(z, mu, w0, w2, a0, a2, g2, k_k, k_a, r_k, norm_g, norm_b, w_out):
    bsz, t, _ = z.shape
    zf = z.astype(jnp.float32)
    zs = zf + mu * (shift_right(zf) - zf)
    r = zs[..., 0:W_RWKV]
    k = zs[..., W_RWKV:2 * W_RWKV]
    v = zs[..., 2 * W_RWKV:3 * W_RWKV]
    o0 = 3 * W_RWKV
    w_lo = zs[..., o0:o0 + RWKV_DECAY_RANK]
    a_lo = zs[..., o0 + RWKV_DECAY_RANK:o0 + RWKV_DECAY_RANK + RWKV_A_RANK]
    g_lo = zs[..., o0 + RWKV_DECAY_RANK + RWKV_A_RANK:RWKV_IN]
    w = -jax.nn.softplus(-(w0 + jnp.tanh(w_lo) @ w2)) - 0.5
    decay = jnp.exp(-jnp.exp(w))
    a = jax.nn.sigmoid(a0 + a_lo @ a2)
    g = jax.nn.sigmoid(g_lo) @ g2

    def heads(y):
        return y.reshape(bsz, t, RWKV_HEADS, RWKV_HEAD_DIM)

    kk = heads(k * k_k)
    kk = kk / jnp.maximum(jnp.sqrt(jnp.sum(kk * kk, -1, keepdims=True)), 1e-12)
    k = k * (1.0 + (a - 1.0) * k_a)
    rh, kh, vh, ah, wh = heads(r), heads(k), heads(v), heads(a), heads(decay)

    def tmajor(y):
        return jnp.moveaxis(y, 1, 0)

    def step(S, inp):
        r_t, w_t, k_t, v_t, kk_t, a_t = inp
        sa = jnp.einsum('bhvk,bhk->bhv', S, -kk_t)
        S = (S * w_t[:, :, None, :] + sa[..., None] * (kk_t * a_t)[:, :, None, :]
             + v_t[..., None] * k_t[:, :, None, :])
        return S, jnp.einsum('bhvk,bhk->bhv', S, r_t)

    init = jnp.zeros((bsz, RWKV_HEADS, RWKV_HEAD_DIM, RWKV_HEAD_DIM), jnp.float32)
    xs = (tmajor(rh), tmajor(wh), tmajor(kh), tmajor(vh), tmajor(kk), tmajor(ah))
    _, y = lax.scan(step, init, xs)
    y = jnp.moveaxis(y, 0, 1).reshape(bsz, t, W_RWKV)
    y = head_norm(y, norm_g, norm_b, RWKV_HEADS, RWKV_GN_EPS)
    bonus = jnp.sum(rh * kh * r_k.reshape(RWKV_HEADS, RWKV_HEAD_DIM), -1, keepdims=True) * vh
    o = (y + bonus.reshape(bsz, t, W_RWKV)) * g
    return o.astype(z.dtype) @ w_out


def conv_ffn(x, w_up, w_conv, w_down):
    h = causal_dwconv(x @ w_up, w_conv)
    return (jax.nn.silu(h[..., :D_FF]) * h[..., D_FF:]) @ w_down


def setup_inputs(seed: int = 0) -> dict:
    key = jax.random.key(seed)
    ks = jax.random.split(key, 40)
    f32 = jnp.float32
    L = DEPTH

    def nrm(k, shape, scale):
        return jax.random.normal(k, shape, f32) * scale

    def unif(k, shape, lo, hi):
        return jax.random.uniform(k, shape, f32, lo, hi)

    inp = {}
    inp["x"] = nrm(ks[0], (BATCH, SEQ, D_MODEL), 1.0)
    inp["positions"] = jnp.broadcast_to(jnp.arange(SEQ, dtype=jnp.int32), (BATCH, SEQ))
    inp["w_in"] = nrm(ks[1], (L, D_MODEL, N_IN), D_MODEL ** -0.5)
    inp["ssm_lambda_re"] = -0.5 + nrm(ks[2], (L, SSM_GROUPS, SSM_STATE), 0.01)
    inp["ssm_lambda_im"] = (math.pi * jnp.arange(SSM_STATE, dtype=f32)
                            + nrm(ks[3], (L, SSM_GROUPS, SSM_STATE), 0.01))
    inp["ssm_log_step"] = unif(ks[4], (L, SSM_GROUPS), math.log(SSM_STEP_MIN), math.log(SSM_STEP_MAX))
    inp["ssm_b_re"] = nrm(ks[5], (L, SSM_GROUPS, SSM_STATE, SSM_GROUP), (2.0 * SSM_GROUP) ** -0.5)
    inp["ssm_b_im"] = nrm(ks[6], (L, SSM_GROUPS, SSM_STATE, SSM_GROUP), (2.0 * SSM_GROUP) ** -0.5)
    inp["ssm_c_re"] = nrm(ks[7], (L, SSM_GROUPS, SSM_GROUP, SSM_STATE), (2.0 / SSM_STATE) ** 0.5)
    inp["ssm_c_im"] = nrm(ks[8], (L, SSM_GROUPS, SSM_GROUP, SSM_STATE), (2.0 / SSM_STATE) ** 0.5)
    inp["ssm_d"] = nrm(ks[9], (L, W_SSM), 1.0)
    inp["ssm_glu"] = nrm(ks[10], (L, W_SSM, 2 * D_MODEL), W_SSM ** -0.5)
    inp["ret_norm_g"] = 1.0 + nrm(ks[11], (L, W_RET), 0.02)
    inp["ret_norm_b"] = nrm(ks[12], (L, W_RET), 0.02)
    inp["ret_out"] = nrm(ks[13], (L, W_RET, D_MODEL), W_RET ** -0.5)
    inp["rwkv_mu"] = unif(ks[14], (L, RWKV_IN), 0.0, 1.0)
    inp["rwkv_w0"] = unif(ks[15], (L, W_RWKV), -6.0, -1.0)
    inp["rwkv_w2"] = nrm(ks[16], (L, RWKV_DECAY_RANK, W_RWKV), 0.1 * RWKV_DECAY_RANK ** -0.5)
    inp["rwkv_a0"] = nrm(ks[17], (L, W_RWKV), 0.1)
    inp["rwkv_a2"] = nrm(ks[18], (L, RWKV_A_RANK, W_RWKV), RWKV_A_RANK ** -0.5)
    inp["rwkv_g2"] = nrm(ks[19], (L, RWKV_GATE_RANK, W_RWKV), RWKV_GATE_RANK ** -0.5)
    inp["rwkv_k_k"] = 0.85 + nrm(ks[20], (L, W_RWKV), 0.05)
    inp["rwkv_k_a"] = 1.0 + nrm(ks[21], (L, W_RWKV), 0.05)
    inp["rwkv_r_k"] = nrm(ks[22], (L, W_RWKV), 0.1)
    inp["rwkv_norm_g"] = 1.0 + nrm(ks[23], (L, W_RWKV), 0.02)
    inp["rwkv_norm_b"] = nrm(ks[24], (L, W_RWKV), 0.02)
    inp["rwkv_out"] = nrm(ks[25], (L, W_RWKV, D_MODEL), W_RWKV ** -0.5)
    inp["w_o"] = nrm(ks[26], (L, D_MODEL, D_MODEL), DEEPNORM_BETA * D_MODEL ** -0.5)
    inp["ln1_g"] = 1.0 + nrm(ks[27], (L, D_MODEL), 0.02)
    inp["ln1_b"] = nrm(ks[28], (L, D_MODEL), 0.02)
    inp["ffn_up"] = nrm(ks[29], (L, D_MODEL, 2 * D_FF), D_MODEL ** -0.5)
    inp["ffn_conv"] = nrm(ks[30], (L, CONV_WIDTH, 2 * D_FF), CONV_WIDTH ** -0.5)
    inp["ffn_down"] = nrm(ks[31], (L, D_FF, D_MODEL), DEEPNORM_BETA * D_FF ** -0.5)
    inp["ln2_g"] = 1.0 + nrm(ks[32], (L, D_MODEL), 0.02)
    inp["ln2_b"] = nrm(ks[33], (L, D_MODEL), 0.02)
    return inp


def reference(x, positions, w_in, ssm_lambda_re, ssm_lambda_im, ssm_log_step, ssm_b_re, ssm_b_im,
              ssm_c_re, ssm_c_im, ssm_d, ssm_glu, ret_norm_g, ret_norm_b, ret_out, rwkv_mu, rwkv_w0,
              rwkv_w2, rwkv_a0, rwkv_a2, rwkv_g2, rwkv_k_k, rwkv_k_a, rwkv_r_k, rwkv_norm_g, rwkv_norm_b,
              rwkv_out, w_o, ln1_g, ln1_b, ffn_up, ffn_conv, ffn_down, ln2_g, ln2_b):
    bsz, t, _ = x.shape
    for l in range(DEPTH):
        proj = x @ w_in[l]
        y_ssm = s5_branch(proj[..., OFF_SSM:OFF_RET], ssm_lambda_re[l], ssm_lambda_im[l],
                          ssm_log_step[l], ssm_b_re[l], ssm_b_im[l], ssm_c_re[l], ssm_c_im[l],
                          ssm_d[l], ssm_glu[l])
        y_ret = retention_branch(proj[..., OFF_RET:OFF_RET + W_RET],
                                 proj[..., OFF_RET + W_RET:OFF_RET + 2 * W_RET],
                                 proj[..., OFF_RET + 2 * W_RET:OFF_RET + 3 * W_RET],
                                 proj[..., OFF_RET + 3 * W_RET:OFF_RWKV],
                                 positions, ret_norm_g[l], ret_norm_b[l], ret_out[l])
        y_rwkv = rwkv7_branch(proj[..., OFF_RWKV:OFF_GATE], rwkv_mu[l], rwkv_w0[l], rwkv_w2[l],
                              rwkv_a0[l], rwkv_a2[l], rwkv_g2[l], rwkv_k_k[l], rwkv_k_a[l],
                              rwkv_r_k[l], rwkv_norm_g[l], rwkv_norm_b[l], rwkv_out[l])
        gates = jax.nn.sigmoid(proj[..., OFF_GATE:].astype(jnp.float32)).reshape(bsz, t, N_BRANCH, D_MODEL)
        merged = gates[:, :, 0] * y_ssm + gates[:, :, 1] * y_ret + gates[:, :, 2] * y_rwkv
        x = layer_norm(DEEPNORM_ALPHA * x + merged.astype(x.dtype) @ w_o[l], ln1_g[l], ln1_b[l])
        x = layer_norm(DEEPNORM_ALPHA * x + conv_ffn(x, ffn_up[l], ffn_conv[l], ffn_down[l]),
                       ln2_g[l], ln2_b[l])
    return x
```

```python
import functools
import math

import jax
import jax.numpy as jnp
from jax import lax
from jax.experimental import pallas as pl
from jax.experimental.pallas import tpu as pltpu

F32 = jnp.float32
BF16 = jnp.bfloat16

D_MODEL = 2048
DEPTH = 4
W_BR = 1024
SSM_GROUP = 16
SSM_GROUPS = 64
SSM_STATE = 64
SSM_CHUNK = 16
SSM_GB = 8
RET_HEADS = 8
RET_DH = 128
RET_CHUNK = 128
ROPE_BASE = 10000.0
RWKV_HEADS = 16
RWKV_DH = 64
RWKV_CHUNK = 64
RWKV_IN = 3328
D_FF = 5632
OFF_RET = 1024
OFF_RWKV = 5120
OFF_GATE = 8448
N_GATE = 3 * D_MODEL
DEEPNORM_ALPHA = (2.0 * DEPTH) ** 0.25
LN_EPS = 1e-5
GN_EPS = 1e-5
RWKV_GN_EPS = 64e-5
MIB = 1024 * 1024


def _params(sem, vmem_mib):
    return pltpu.CompilerParams(dimension_semantics=sem, vmem_limit_bytes=vmem_mib * MIB)


def _dot(a, b):
    return jnp.dot(a, b, preferred_element_type=F32)


def _dot_nt(a, b):
    return lax.dot_general(a, b, (((1,), (1,)), ((), ())), preferred_element_type=F32)


def _dot_tn(a, b):
    return lax.dot_general(a, b, (((0,), (0,)), ((), ())), preferred_element_type=F32)


def _split2(x):
    hi = x.astype(BF16)
    lo = (x - hi.astype(F32)).astype(BF16)
    return hi, lo


def _split3(x):
    hi = x.astype(BF16)
    r1 = x - hi.astype(F32)
    mid = r1.astype(BF16)
    lo = (r1 - mid.astype(F32)).astype(BF16)
    return hi, mid, lo


def _layer_norm(y, g, b):
    mu = jnp.mean(y, axis=-1, keepdims=True)
    d = y - mu
    var = jnp.mean(d * d, axis=-1, keepdims=True)
    return d * lax.rsqrt(var + LN_EPS) * g + b


def _proj_kernel(x_ref, w_ref, o_ref, *, act):
    acc = _dot(x_ref[...], w_ref[...])
    if act == "sigmoid":
        acc = jax.nn.sigmoid(acc)
    o_ref[...] = acc.astype(o_ref.dtype)


def _proj(x_bf, w_all, layer, col0, ncols, tn, act=None):
    t, k = x_bf.shape
    tm = min(1024, t)
    off = col0 // tn
    assert off * tn == col0 and ncols % tn == 0
    return pl.pallas_call(
        functools.partial(_proj_kernel, act=act),
        grid=(t // tm, ncols // tn),
        in_specs=[pl.BlockSpec((tm, k), lambda i, j: (i, 0)),
                  pl.BlockSpec((None, k, tn), lambda i, j: (layer, 0, j + off))],
        out_specs=pl.BlockSpec((tm, tn), lambda i, j: (i, j)),
        out_shape=jax.ShapeDtypeStruct((t, ncols), F32),
        compiler_params=_params(("parallel", "arbitrary"), 40),
        name="proj",
    )(x_bf, w_all)


def _rope_kernel(pos_ref, cos_ref, sin_ref):
    half = RET_DH // 2
    pos = pos_ref[...].astype(F32)
    lane = lax.broadcasted_iota(jnp.int32, (1, RET_DH), 1)
    idx = jnp.where(lane < half, lane, lane - half).astype(F32)
    inv_freq = jnp.exp(-(idx / half) * math.log(ROPE_BASE))
    ang = pos * inv_freq
    cos_ref[...] = jnp.cos(ang)
    s = jnp.sin(ang)
    sin_ref[...] = jnp.where(lane < half, -s, s)


def _rope_tables(positions):
    t = positions.shape[-1]
    tm = min(1024, t)
    pos = positions.reshape(t, 1)
    return pl.pallas_call(
        _rope_kernel,
        grid=(t // tm,),
        in_specs=[pl.BlockSpec((tm, 1), lambda i: (i, 0))],
        out_specs=[pl.BlockSpec((tm, RET_DH), lambda i: (i, 0))] * 2,
        out_shape=[jax.ShapeDtypeStruct((t, RET_DH), F32)] * 2,
        compiler_params=_params(("parallel",), 16),
        name="rope",
    )(pos)


_RET_LOG_GAMMA = [math.log1p(-(2.0 ** (-5.0 - h))) for h in range(RET_HEADS)]


def _ret_kernel(q_ref, k_ref, v_ref, g_ref, cos_ref, sin_ref, ng_ref, nb_ref, o_ref, state_ref):
    c = RET_CHUNK

    @pl.when(pl.program_id(0) == 0)
    def _():
        state_ref[...] = jnp.zeros_like(state_ref)

    cos = cos_ref[...]
    sin = sin_ref[...]
    ri = lax.broadcasted_iota(jnp.int32, (c, c), 0).astype(F32)
    ci = lax.broadcasted_iota(jnp.int32, (c, c), 1).astype(F32)
    rel = ri - ci
    scale = RET_DH ** -0.5
    for h in range(RET_HEADS):
        sl = slice(h * RET_DH, (h + 1) * RET_DH)
        lg = _RET_LOG_GAMMA[h]
        q = q_ref[:, sl]
        k = k_ref[:, sl]
        qr = q * cos + pltpu.roll(q, RET_DH // 2, 1) * sin
        kr = (k * cos + pltpu.roll(k, RET_DH // 2, 1) * sin) * scale
        v_bf = v_ref[:, sl].astype(BF16)
        decay = jnp.where(rel >= 0, jnp.exp(lg * jnp.maximum(rel, 0.0)), 0.0)
        q_bf = qr.astype(BF16)
        scores = _dot_nt(q_bf, kr.astype(BF16)) * decay
        inner = _dot(scores.astype(BF16), v_bf)
        state = state_ref[h]
        cross = _dot(q_bf, state.astype(BF16)) * jnp.exp(lg * (ri + 1.0))
        k_dec = (kr * jnp.exp(lg * (c - 1.0 - ri))).astype(BF16)
        state_ref[h] = math.exp(lg * c) * state + _dot_tn(k_dec, v_bf)
        o = inner + cross
        mu = jnp.mean(o, axis=-1, keepdims=True)
        d = o - mu
        var = jnp.mean(d * d, axis=-1, keepdims=True)
        on = d * lax.rsqrt(var + GN_EPS) * ng_ref[:, sl] + nb_ref[:, sl]
        g = g_ref[:, sl]
        o_ref[:, sl] = (g * jax.nn.sigmoid(g) * on).astype(BF16)


def _retention(ret, cos_t, sin_t, ng, nb, layer):
    t = ret.shape[0]
    c = RET_CHUNK
    col = lambda j: pl.BlockSpec((c, W_BR), lambda i, j=j: (i, j))
    vec = pl.BlockSpec((None, 1, W_BR), lambda i: (layer, 0, 0))
    tab = pl.BlockSpec((c, RET_DH), lambda i: (i, 0))
    return pl.pallas_call(
        _ret_kernel,
        grid=(t // c,),
        in_specs=[col(0), col(1), col(2), col(3), tab, tab, vec, vec],
        out_specs=pl.BlockSpec((c, W_BR), lambda i: (i, 0)),
        out_shape=jax.ShapeDtypeStruct((t, W_BR), BF16),
        scratch_shapes=[pltpu.VMEM((RET_HEADS, RET_DH, RET_DH), F32)],
        compiler_params=_params(("arbitrary",), 32),
        name="retention",
    )(ret, ret, ret, ret, cos_t, sin_t, ng, nb)


def _seg64_sum(x, block_ones):
    hi, lo = _split2(x)
    outs = []
    for t in range(x.shape[1] // 128):
        sl = slice(t * 128, (t + 1) * 128)
        outs.append(_dot(hi[:, sl], block_ones) + _dot(lo[:, sl], block_ones))
    return jnp.concatenate(outs, axis=1)


def _rwkv_kernel(z_ref, mu_ref, w0_ref, a0_ref, kk_ref, ka_ref, rk_ref, ng_ref, nb_ref,
                 w2_ref, a2_ref, g2_ref, o_ref, last_ref, s_ref):
    L = RWKV_CHUNK
    P = 2 * RWKV_DH

    @pl.when(pl.program_id(0) == 0)
    def _():
        last_ref[...] = jnp.zeros_like(last_ref)
        s_ref[...] = jnp.zeros_like(s_ref)

    z = z_ref[...]
    rows = lax.broadcasted_iota(jnp.int32, z.shape, 0)
    zprev = jnp.where(rows == 0, last_ref[...], pltpu.roll(z, 1, 0))
    last_ref[...] = z[L - 1:L, :]
    zs = z + mu_ref[...] * (zprev - z)
    r = zs[:, 0:W_BR]
    k = zs[:, W_BR:2 * W_BR]
    v = zs[:, 2 * W_BR:3 * W_BR]
    w_lo = zs[:, 3 * W_BR:3 * W_BR + 64]
    a_lo = zs[:, 3 * W_BR + 64:3 * W_BR + 128]
    g_lo = zs[:, 3 * W_BR + 128:3 * W_BR + 256]

    wpre = w0_ref[...] + _dot(jnp.tanh(w_lo).astype(BF16), w2_ref[...])
    nw = -wpre
    softplus = jnp.maximum(nw, 0.0) + jnp.log1p(jnp.exp(-jnp.abs(nw)))
    w = -softplus - 0.5
    lw = -jnp.exp(w)
    a = jax.nn.sigmoid(a0_ref[...] + _dot(a_lo.astype(BF16), a2_ref[...]))
    g = _dot(jax.nn.sigmoid(g_lo).astype(BF16), g2_ref[...])

    li = lax.broadcasted_iota(jnp.int32, (P, P), 0)
    lj = lax.broadcasted_iota(jnp.int32, (P, P), 1)
    block_ones = jnp.where((li // RWKV_DH) == (lj // RWKV_DH), 1.0, 0.0).astype(BF16)

    kkv = k * kk_ref[...]
    ss = _seg64_sum(kkv * kkv, block_ones)
    kkn = kkv / jnp.maximum(jnp.sqrt(ss), 1e-12)
    k2 = k * (1.0 + (a - 1.0) * ka_ref[...])
    b = kkn * a

    ti = lax.broadcasted_iota(jnp.int32, (L, L), 0)
    tj = lax.broadcasted_iota(jnp.int32, (L, L), 1)
    tri = jnp.where(ti >= tj, 1.0, 0.0).astype(BF16)
    l_hi, l_mid, l_lo = _split3(lw)
    cum = _dot(tri, l_hi) + _dot(tri, l_mid) + _dot(tri, l_lo)
    cum_last = cum[L - 1:L, :]
    e_to_end = jnp.exp(cum_last - cum)
    e_inv = jnp.exp(-cum)
    kkd = kkn * jnp.exp(cum - lw)
    rd = r * jnp.exp(cum)
    bd = b * e_inv
    kd = k2 * e_inv
    bd2 = b * e_to_end
    kd2 = k2 * e_to_end
    p_last = jnp.exp(cum_last)

    tt = li % L
    ts = lj % L
    strict = tt > ts
    incl = tt >= ts
    eye = jnp.where(li == lj, 1.0, 0.0)
    first_half = lax.broadcasted_iota(jnp.int32, (L, P), 1) < RWKV_DH

    def stack(x, sl):
        xp = x[:, sl]
        return jnp.concatenate([jnp.where(first_half, xp, 0.0), jnp.where(first_half, 0.0, xp)],
                               axis=0).astype(BF16)

    ys = []
    for p in range(RWKV_HEADS // 2):
        sl = slice(p * P, (p + 1) * P)
        kkm, rm, bm, km = stack(kkd, sl), stack(rd, sl), stack(bd, sl), stack(kd, sl)
        bm2, km2, vm = stack(bd2, sl), stack(kd2, sl), stack(v, sl)
        gram = _dot_nt(jnp.concatenate([kkm, rm], axis=0), jnp.concatenate([bm, km], axis=0))
        a_bb = jnp.where(strict, gram[:P, :P], 0.0)
        a_bk = jnp.where(strict, gram[:P, P:], 0.0)
        r_b = jnp.where(incl, gram[P:, :P], 0.0)
        r_k = jnp.where(incl, gram[P:, P:], 0.0)
        t_inv = eye - jnp.where((tt // 2) == (ts // 2), a_bb, 0.0)
        blk = 2
        while blk < L:
            m = jnp.where(((tt // (2 * blk)) == (ts // (2 * blk))) & ((tt // blk) != (ts // blk)), a_bb, 0.0)
            t_bf = t_inv.astype(BF16)
            t_inv = t_inv - _dot(t_bf, _dot(m.astype(BF16), t_bf).astype(BF16))
            blk *= 2
        s_old = s_ref[p]
        s_bf = s_old.astype(BF16)
        wm = _dot_nt(kkm, s_bf) + _dot(a_bk.astype(BF16), vm)
        u_bf = (-_dot(t_inv.astype(BF16), wm.astype(BF16))).astype(BF16)
        uv = jnp.concatenate([u_bf, vm], axis=0)
        y = _dot_nt(rm, s_bf) + _dot(jnp.concatenate([r_b, r_k], axis=1).astype(BF16), uv)
        s_ref[p] = s_old * p_last[:, sl] + _dot_tn(uv, jnp.concatenate([bm2, km2], axis=0))
        ys.append(y[:L] + y[L:])
    y = jnp.concatenate(ys, axis=1)

    inv_n = 1.0 / RWKV_DH
    mu = _seg64_sum(y, block_ones) * inv_n
    d = y - mu
    var = _seg64_sum(d * d, block_ones) * inv_n
    yn = d * lax.rsqrt(var + RWKV_GN_EPS) * ng_ref[...] + nb_ref[...]
    bonus = _seg64_sum(r * k2 * rk_ref[...], block_ones) * v
    o_ref[...] = ((yn + bonus) * g).astype(BF16)


def _rwkv(rz, prm, layer):
    t = rz.shape[0]
    L = RWKV_CHUNK
    vec = lambda n: pl.BlockSpec((None, 1, n), lambda i: (layer, 0, 0))
    mat = lambda r: pl.BlockSpec((None, r, W_BR), lambda i: (layer, 0, 0))
    return pl.pallas_call(
        _rwkv_kernel,
        grid=(t // L,),
        in_specs=[pl.BlockSpec((L, RWKV_IN), lambda i: (i, 0)), vec(RWKV_IN)] + [vec(W_BR)] * 7
                 + [mat(64), mat(64), mat(128)],
        out_specs=pl.BlockSpec((L, W_BR), lambda i: (i, 0)),
        out_shape=jax.ShapeDtypeStruct((t, W_BR), BF16),
        scratch_shapes=[pltpu.VMEM((1, RWKV_IN), F32),
                        pltpu.VMEM((RWKV_HEADS // 2, 2 * RWKV_DH, 2 * RWKV_DH), F32)],
        compiler_params=_params(("arbitrary",), 32),
        name="rwkv7",
    )(rz, prm["mu"], prm["w0"], prm["a0"], prm["k_k"], prm["k_a"], prm["r_k"], prm["ng"], prm["nb"],
      prm["w2"], prm["a2"], prm["g2"])


def _gelu_tanh(x):
    c = math.sqrt(2.0 / math.pi)
    return x * (0.5 * (1.0 + jnp.tanh(c * (x + 0.044715 * (x * x * x)))))


def _s5_kernel(u_ref, lre_ref, lim_ref, ls_ref, bre_ref, bim_ref, cre_ref, cim_ref, d_ref, z_ref,
               xre_ref, xim_ref, pre_ref, pim_ref, a_ref, m_ref, wo_ref):
    C = SSM_CHUNK
    GW = C * SSM_GROUP
    PS = SSM_STATE
    n = u_ref.shape[0]
    tau = lax.broadcasted_iota(jnp.int32, (24, PS), 0).astype(F32)
    lane = lax.broadcasted_iota(jnp.int32, (SSM_GROUP, GW), 1)

    for g in range(SSM_GB):
        lr = lre_ref[g:g + 1, :]
        li = lim_ref[g:g + 1, :]
        dt = jnp.exp(ls_ref[g:g + 1, :])
        mag = jnp.exp(lr * dt)
        ab_re = mag * jnp.cos(li * dt)
        ab_im = mag * jnp.sin(li * dt)
        denom = lr * lr + li * li
        f_re = ((ab_re - 1.0) * lr + ab_im * li) / denom
        f_im = (ab_im * lr - (ab_re - 1.0) * li) / denom
        b_re = bre_ref[g]
        b_im = bim_ref[g]
        bb_re = f_re * b_re - f_im * b_im
        bb_im = f_re * b_im + f_im * b_re
        pmag = jnp.exp(tau * (lr * dt))
        pw_re = pmag * jnp.cos(tau * (li * dt))
        pw_im = pmag * jnp.sin(tau * (li * dt))
        c_re = cre_ref[g]
        c_im = cim_ref[g]
        e_re, e_nim, f_res, f_ims = [], [], [], []
        for t in range(C + 1):
            pr = pw_re[t:t + 1, :]
            pi = pw_im[t:t + 1, :]
            e_re.append(pr * c_re - pi * c_im)
            e_nim.append(-(pr * c_im + pi * c_re))
        for j in range(C):
            pr = pw_re[C - 1 - j:C - j, :]
            pi = pw_im[C - 1 - j:C - j, :]
            f_res.append(pr * bb_re - pi * bb_im)
            f_ims.append(pr * bb_im + pi * bb_re)
        et_re = jnp.concatenate(e_re, axis=0)
        et_nim = jnp.concatenate(e_nim, axis=0)
        kt = jnp.zeros((SSM_GROUP, GW), F32)
        for x, y in ((bb_re, et_re[:GW]), (bb_im, et_nim[:GW])):
            xh, xl = _split2(x)
            yh, yl = _split2(y)
            kt = kt + _dot_nt(xh, yh) + _dot_nt(xh, yl) + _dot_nt(xl, yh)
        blocks = [kt]
        for j in range(1, C):
            blocks.append(jnp.where(lane >= j * SSM_GROUP, pltpu.roll(kt, j * SSM_GROUP, 1), 0.0))
        m_ref[g] = jnp.concatenate(blocks, axis=0).astype(BF16)
        wo_ref[g, 0] = et_re[SSM_GROUP:].astype(BF16)
        wo_ref[g, 1] = et_nim[SSM_GROUP:].astype(BF16)
        a_ref[0:1, g * PS:(g + 1) * PS] = pw_re[C:C + 1, :]
        a_ref[1:2, g * PS:(g + 1) * PS] = pw_im[C:C + 1, :]
        u_bf = u_ref[:, g * GW:(g + 1) * GW].astype(BF16)
        xre_ref[:, g * PS:(g + 1) * PS] = _dot(u_bf, jnp.concatenate(f_res, axis=0).astype(BF16))
        xim_ref[:, g * PS:(g + 1) * PS] = _dot(u_bf, jnp.concatenate(f_ims, axis=0).astype(BF16))

    a_re = a_ref[0:1, :]
    a_im = a_ref[1:2, :]

    def step(i, carry):
        s_re, s_im = carry
        pre_ref[pl.ds(i, 1), :] = s_re
        pim_ref[pl.ds(i, 1), :] = s_im
        x_re = xre_ref[pl.ds(i, 1), :]
        x_im = xim_ref[pl.ds(i, 1), :]
        return (a_re * s_re - a_im * s_im + x_re, a_re * s_im + a_im * s_re + x_im)

    zero = jnp.zeros((1, SSM_GB * PS), F32)
    lax.fori_loop(0, n, step, (zero, zero), unroll=8)

    for g in range(SSM_GB):
        cols = slice(g * GW, (g + 1) * GW)
        u = u_ref[:, cols]
        y = _dot(u.astype(BF16), m_ref[g])
        y = y + _dot_nt(pre_ref[:, g * PS:(g + 1) * PS].astype(BF16), wo_ref[g, 0])
        y = y + _dot_nt(pim_ref[:, g * PS:(g + 1) * PS].astype(BF16), wo_ref[g, 1])
        y = y + d_ref[:, cols] * u
        z_ref[:, cols] = _gelu_tanh(y).astype(BF16)


def _s5(u2, prm, layer):
    n, width = u2.shape
    gw = SSM_CHUNK * SSM_GROUP * SSM_GB
    lam = pl.BlockSpec((None, SSM_GB, SSM_STATE), lambda i: (layer, i, 0))
    bc = pl.BlockSpec((None, SSM_GB, SSM_GROUP, SSM_STATE), lambda i: (layer, i, 0, 0))
    st = SSM_GB * SSM_STATE
    return pl.pallas_call(
        _s5_kernel,
        grid=(SSM_GROUPS // SSM_GB,),
        in_specs=[pl.BlockSpec((n, gw), lambda i: (0, i)), lam, lam, lam, bc, bc, bc, bc,
                  pl.BlockSpec((None, 1, gw), lambda i: (layer, 0, i))],
        out_specs=pl.BlockSpec((n, gw), lambda i: (0, i)),
        out_shape=jax.ShapeDtypeStruct((n, width), BF16),
        scratch_shapes=[pltpu.VMEM((n, st), F32), pltpu.VMEM((n, st), F32),
                        pltpu.VMEM((n, st), F32), pltpu.VMEM((n, st), F32),
                        pltpu.VMEM((8, st), F32),
                        pltpu.VMEM((SSM_GB, SSM_CHUNK * SSM_GROUP, SSM_CHUNK * SSM_GROUP), BF16),
                        pltpu.VMEM((SSM_GB, 2, SSM_CHUNK * SSM_GROUP, SSM_STATE), BF16)],
        compiler_params=_params(("parallel",), 40),
        name="s5",
    )(u2, prm["lam_re"], prm["lam_im"], prm["log_step"], prm["b_re"], prm["b_im"], prm["c_re"], prm["c_im"],
      prm["d"])


def _merge_kernel(z_ref, r_ref, k_ref, wa_ref, wb_ref, wr_ref, wk_ref, g0_ref, g1_ref, g2_ref, o_ref):
    z = z_ref[...]
    y_ssm = _dot(z, wa_ref[...]) * jax.nn.sigmoid(_dot(z, wb_ref[...]))
    y_ret = _dot(r_ref[...], wr_ref[...])
    y_rwkv = _dot(k_ref[...], wk_ref[...])
    o_ref[...] = (g0_ref[...] * y_ssm + g1_ref[...] * y_ret + g2_ref[...] * y_rwkv).astype(BF16)


def _merge(z, o_ret, o_rwkv, w_glu, w_ret, w_rwkv, gates, layer):
    t = z.shape[0]
    tm = min(1024, t)
    tn = 512
    nb = D_MODEL // tn
    act = pl.BlockSpec((tm, W_BR), lambda i, j: (i, 0))
    wsp = lambda off: pl.BlockSpec((None, W_BR, tn), lambda i, j, off=off: (layer, 0, j + off))
    gsp = lambda off: pl.BlockSpec((tm, tn), lambda i, j, off=off: (i, j + off))
    return pl.pallas_call(
        _merge_kernel,
        grid=(t // tm, nb),
        in_specs=[act, act, act, wsp(0), wsp(nb), wsp(0), wsp(0), gsp(0), gsp(nb), gsp(2 * nb)],
        out_specs=pl.BlockSpec((tm, tn), lambda i, j: (i, j)),
        out_shape=jax.ShapeDtypeStruct((t, D_MODEL), BF16),
        compiler_params=_params(("parallel", "arbitrary"), 48),
        name="merge",
    )(z, o_ret, o_rwkv, w_glu, w_glu, w_ret, w_rwkv, gates, gates, gates)


def _out_ln_kernel(m_ref, w_ref, x_ref, g_ref, b_ref, o_ref, obf_ref):
    y = DEEPNORM_ALPHA * x_ref[...] + _dot(m_ref[...], w_ref[...])
    out = _layer_norm(y, g_ref[...], b_ref[...])
    o_ref[...] = out
    obf_ref[...] = out.astype(BF16)


def _out_ln(m, w_all, x, g, b, layer):
    t = x.shape[0]
    tm = min(256, t)
    row = pl.BlockSpec((tm, D_MODEL), lambda i: (i, 0))
    vec = pl.BlockSpec((None, 1, D_MODEL), lambda i: (layer, 0, 0))
    return pl.pallas_call(
        _out_ln_kernel,
        grid=(t // tm,),
        in_specs=[row, pl.BlockSpec((None, D_MODEL, D_MODEL), lambda i: (layer, 0, 0)), row, vec, vec],
        out_specs=[row, row],
        out_shape=[jax.ShapeDtypeStruct((t, D_MODEL), F32), jax.ShapeDtypeStruct((t, D_MODEL), BF16)],
        compiler_params=_params(("parallel",), 48),
        name="out_ln",
    )(m, w_all, x, g, b)


def _ffn_up_kernel(x_ref, w1_ref, w2_ref, c1_ref, c2_ref, o_ref, h1_ref, h2_ref):
    tm = x_ref.shape[0]

    @pl.when(pl.program_id(1) == 0)
    def _():
        h1_ref[0:8, :] = jnp.zeros((8, h1_ref.shape[1]), F32)
        h2_ref[0:8, :] = jnp.zeros((8, h2_ref.shape[1]), F32)

    x = x_ref[...]

    def conv(w_ref, c_ref, h_ref):
        h = _dot(x, w_ref[...])
        h_ref[8:8 + tm, :] = h
        out = c_ref[2:3, :] * h + c_ref[1:2, :] * h_ref[7:7 + tm, :] + c_ref[0:1, :] * h_ref[6:6 + tm, :]
        h_ref[0:8, :] = h[tm - 8:tm, :]
        return out

    a = conv(w1_ref, c1_ref, h1_ref)
    b = conv(w2_ref, c2_ref, h2_ref)
    o_ref[...] = (a * jax.nn.sigmoid(a) * b).astype(BF16)


def _ffn_up(x_bf, w_up, w_conv, layer):
    t = x_bf.shape[0]
    tm = min(1024, t)
    tn = 512
    nb = D_FF // tn
    wsp = lambda off: pl.BlockSpec((None, D_MODEL, tn), lambda j, i, off=off: (layer, 0, j + off))
    csp = lambda off: pl.BlockSpec((None, 3, tn), lambda j, i, off=off: (layer, 0, j + off))
    return pl.pallas_call(
        _ffn_up_kernel,
        grid=(nb, t // tm),
        in_specs=[pl.BlockSpec((tm, D_MODEL), lambda j, i: (i, 0)), wsp(0), wsp(nb), csp(0), csp(nb)],
        out_specs=pl.BlockSpec((tm, tn), lambda j, i: (i, j)),
        out_shape=jax.ShapeDtypeStruct((t, D_FF), BF16),
        scratch_shapes=[pltpu.VMEM((tm + 8, tn), F32), pltpu.VMEM((tm + 8, tn), F32)],
        compiler_params=_params(("parallel", "arbitrary"), 48),
        name="ffn_up",
    )(x_bf, w_up, w_up, w_conv, w_conv)


def _ffn_down_kernel(a_ref, w_ref, x_ref, g_ref, b_ref, o_ref, obf_ref, acc_ref):
    kk = pl.program_id(1)

    @pl.when(kk == 0)
    def _():
        acc_ref[...] = DEEPNORM_ALPHA * x_ref[...]

    acc_ref[...] += _dot(a_ref[...], w_ref[...])

    @pl.when(kk == pl.num_programs(1) - 1)
    def _():
        out = _layer_norm(acc_ref[...], g_ref[...], b_ref[...])
        o_ref[...] = out
        obf_ref[...] = out.astype(BF16)


def _ffn_down(act, w_all, x, g, b, layer):
    t = x.shape[0]
    tm = min(512, t)
    tk = 512
    row = pl.BlockSpec((tm, D_MODEL), lambda i, k: (i, 0))
    vec = pl.BlockSpec((None, 1, D_MODEL), lambda i, k: (layer, 0, 0))
    return pl.pallas_call(
        _ffn_down_kernel,
        grid=(t // tm, D_FF // tk),
        in_specs=[pl.BlockSpec((tm, tk), lambda i, k: (i, k)),
                  pl.BlockSpec((None, tk, D_MODEL), lambda i, k: (layer, k, 0)), row, vec, vec],
        out_specs=[row, row],
        out_shape=[jax.ShapeDtypeStruct((t, D_MODEL), F32), jax.ShapeDtypeStruct((t, D_MODEL), BF16)],
        scratch_shapes=[pltpu.VMEM((tm, D_MODEL), F32)],
        compiler_params=_params(("parallel", "arbitrary"), 48),
        name="ffn_down",
    )(act, w_all, x, g, b)


def kernel(x, positions, w_in, ssm_lambda_re, ssm_lambda_im, ssm_log_step, ssm_b_re, ssm_b_im, ssm_c_re, ssm_c_im, ssm_d, ssm_glu, ret_norm_g, ret_norm_b, ret_out, rwkv_mu, rwkv_w0, rwkv_w2, rwkv_a0, rwkv_a2, rwkv_g2, rwkv_k_k, rwkv_k_a, rwkv_r_k, rwkv_norm_g, rwkv_norm_b, rwkv_out, w_o, ln1_g, ln1_b, ffn_up, ffn_conv, ffn_down, ln2_g, ln2_b):
    bsz, t, _ = x.shape
    assert bsz == 1
    depth = w_in.shape[0]
    nchunk = t // SSM_CHUNK
    vec3 = lambda p: p.reshape(depth, 1, p.shape[-1])

    w_in_bf = w_in.astype(BF16)
    glu_bf = ssm_glu.astype(BF16)
    ret_out_bf = ret_out.astype(BF16)
    rwkv_out_bf = rwkv_out.astype(BF16)
    w_o_bf = w_o.astype(BF16)
    up_bf = ffn_up.astype(BF16)
    down_bf = ffn_down.astype(BF16)

    s5_prm = dict(
        lam_re=ssm_lambda_re, lam_im=ssm_lambda_im,
        log_step=jnp.broadcast_to(ssm_log_step[..., None], ssm_lambda_re.shape),
        b_re=jnp.swapaxes(ssm_b_re, -1, -2), b_im=jnp.swapaxes(ssm_b_im, -1, -2),
        c_re=ssm_c_re, c_im=ssm_c_im,
        d=jnp.tile(ssm_d.reshape(depth, SSM_GROUPS, 1, SSM_GROUP), (1, 1, SSM_CHUNK, 1)).reshape(depth, 1, -1))
    rwkv_prm = dict(mu=vec3(rwkv_mu), w0=vec3(rwkv_w0), a0=vec3(rwkv_a0), k_k=vec3(rwkv_k_k), k_a=vec3(rwkv_k_a),
                    r_k=vec3(rwkv_r_k), ng=vec3(rwkv_norm_g), nb=vec3(rwkv_norm_b),
                    w2=rwkv_w2.astype(BF16), a2=rwkv_a2.astype(BF16), g2=rwkv_g2.astype(BF16))
    ret_g, ret_b = vec3(ret_norm_g), vec3(ret_norm_b)
    ln1g, ln1b, ln2g, ln2b = vec3(ln1_g), vec3(ln1_b), vec3(ln2_g), vec3(ln2_b)

    cos_t, sin_t = _rope_tables(positions)
    xf = x.reshape(t, D_MODEL)
    xb = xf.astype(BF16)
    for l in range(depth):
        u = _proj(xb, w_in_bf, l, 0, W_BR, 512)
        ret = _proj(xb, w_in_bf, l, OFF_RET, 4 * W_BR, 1024)
        rz = _proj(xb, w_in_bf, l, OFF_RWKV, RWKV_IN, 256)
        gates = _proj(xb, w_in_bf, l, OFF_GATE, N_GATE, 768, act="sigmoid")
        u2 = u.reshape(nchunk, SSM_CHUNK, SSM_GROUPS, SSM_GROUP).transpose(0, 2, 1, 3).reshape(nchunk, -1)
        z2 = _s5(u2, s5_prm, l)
        z = z2.reshape(nchunk, SSM_GROUPS, SSM_CHUNK, SSM_GROUP).transpose(0, 2, 1, 3).reshape(t, W_BR)
        o_ret = _retention(ret, cos_t, sin_t, ret_g, ret_b, l)
        o_rwkv = _rwkv(rz, rwkv_prm, l)
        merged = _merge(z, o_ret, o_rwkv, glu_bf, ret_out_bf, rwkv_out_bf, gates, l)
        xf, xb = _out_ln(merged, w_o_bf, xf, ln1g, ln1b, l)
        act = _ffn_up(xb, up_bf, ffn_conv, l)
        xf, xb = _ffn_down(act, down_bf, xf, ln2g, ln2b, l)
    return xf.reshape(bsz, t, D_MODEL)
```

```python
import functools
import math

import jax
import jax.numpy as jnp
from jax import lax
from jax.experimental import pallas as pl
from jax.experimental.pallas import tpu as pltpu

F32 = jnp.float32
BF16 = jnp.bfloat16

D_MODEL = 2048
DEPTH = 4
W_BR = 1024
SSM_GROUP = 16
SSM_GROUPS = 64
SSM_STATE = 64
SSM_CHUNK = 16
SSM_GB = 8
RET_HEADS = 8
RET_DH = 128
RET_CHUNK = 128
ROPE_BASE = 10000.0
RWKV_HEADS = 16
RWKV_DH = 64
RWKV_CHUNK = 64
RWKV_ROWS = 256
RWKV_IN = 3328
D_FF = 5632
OFF_RET = 1024
OFF_RWKV = 5120
OFF_GATE = 8448
N_GATE = 3 * D_MODEL
DEEPNORM_ALPHA = (2.0 * DEPTH) ** 0.25
LN_EPS = 1e-5
GN_EPS = 1e-5
RWKV_GN_EPS = 64e-5
MIB = 1024 * 1024


def _params(sem, vmem_mib):
    return pltpu.CompilerParams(dimension_semantics=sem, vmem_limit_bytes=vmem_mib * MIB)


def _dot(a, b):
    return jnp.dot(a, b, preferred_element_type=F32)


def _dot_nt(a, b):
    return lax.dot_general(a, b, (((1,), (1,)), ((), ())), preferred_element_type=F32)


def _dot_tn(a, b):
    return lax.dot_general(a, b, (((0,), (0,)), ((), ())), preferred_element_type=F32)


def _split2(x):
    hi = x.astype(BF16)
    lo = (x - hi.astype(F32)).astype(BF16)
    return hi, lo


def _split3(x):
    hi = x.astype(BF16)
    r1 = x - hi.astype(F32)
    mid = r1.astype(BF16)
    lo = (r1 - mid.astype(F32)).astype(BF16)
    return hi, mid, lo


def _layer_norm(y, g, b):
    mu = jnp.mean(y, axis=-1, keepdims=True)
    d = y - mu
    var = jnp.mean(d * d, axis=-1, keepdims=True)
    return d * lax.rsqrt(var + LN_EPS) * g + b


def _proj_kernel(x_ref, w_ref, o_ref, *, act):
    acc = _dot(x_ref[...], w_ref[...])
    if act == "sigmoid":
        acc = jax.nn.sigmoid(acc)
    o_ref[...] = acc.astype(o_ref.dtype)


def _proj(x_bf, w_all, layer, col0, ncols, tn, act=None):
    t, k = x_bf.shape
    tm = min(1024, t)
    off = col0 // tn
    assert off * tn == col0 and ncols % tn == 0
    return pl.pallas_call(
        functools.partial(_proj_kernel, act=act),
        grid=(t // tm, ncols // tn),
        in_specs=[pl.BlockSpec((tm, k), lambda i, j: (i, 0)),
                  pl.BlockSpec((None, k, tn), lambda i, j: (layer, 0, j + off))],
        out_specs=pl.BlockSpec((tm, tn), lambda i, j: (i, j)),
        out_shape=jax.ShapeDtypeStruct((t, ncols), F32),
        compiler_params=_params(("parallel", "arbitrary"), 40),
        name="proj",
    )(x_bf, w_all)


def _rope_kernel(pos_ref, cos_ref, sin_ref):
    half = RET_DH // 2
    pos = pos_ref[...].astype(F32)
    lane = lax.broadcasted_iota(jnp.int32, (1, RET_DH), 1)
    idx = jnp.where(lane < half, lane, lane - half).astype(F32)
    inv_freq = jnp.exp(-(idx / half) * math.log(ROPE_BASE))
    ang = pos * inv_freq
    cos_ref[...] = jnp.cos(ang)
    s = jnp.sin(ang)
    sin_ref[...] = jnp.where(lane < half, -s, s)


def _rope_tables(positions):
    t = positions.shape[-1]
    tm = min(1024, t)
    pos = positions.reshape(t, 1)
    return pl.pallas_call(
        _rope_kernel,
        grid=(t // tm,),
        in_specs=[pl.BlockSpec((tm, 1), lambda i: (i, 0))],
        out_specs=[pl.BlockSpec((tm, RET_DH), lambda i: (i, 0))] * 2,
        out_shape=[jax.ShapeDtypeStruct((t, RET_DH), F32)] * 2,
        compiler_params=_params(("parallel",), 16),
        name="rope",
    )(pos)


_RET_LOG_GAMMA = [math.log1p(-(2.0 ** (-5.0 - h))) for h in range(RET_HEADS)]


def _ret_kernel(q_ref, k_ref, v_ref, g_ref, cos_ref, sin_ref, ng_ref, nb_ref, o_ref, state_ref):
    c = RET_CHUNK

    @pl.when(pl.program_id(0) == 0)
    def _():
        state_ref[...] = jnp.zeros_like(state_ref)

    cos = cos_ref[...]
    sin = sin_ref[...]
    ri = lax.broadcasted_iota(jnp.int32, (c, c), 0).astype(F32)
    ci = lax.broadcasted_iota(jnp.int32, (c, c), 1).astype(F32)
    rel = ri - ci
    scale = RET_DH ** -0.5
    for h in range(RET_HEADS):
        sl = slice(h * RET_DH, (h + 1) * RET_DH)
        lg = _RET_LOG_GAMMA[h]
        q = q_ref[:, sl]
        k = k_ref[:, sl]
        qr = q * cos + pltpu.roll(q, RET_DH // 2, 1) * sin
        kr = (k * cos + pltpu.roll(k, RET_DH // 2, 1) * sin) * scale
        v_bf = v_ref[:, sl].astype(BF16)
        decay = jnp.where(rel >= 0, jnp.exp(lg * jnp.maximum(rel, 0.0)), 0.0)
        q_bf = qr.astype(BF16)
        scores = _dot_nt(q_bf, kr.astype(BF16)) * decay
        inner = _dot(scores.astype(BF16), v_bf)
        state = state_ref[h]
        cross = _dot(q_bf, state.astype(BF16)) * jnp.exp(lg * (ri + 1.0))
        k_dec = (kr * jnp.exp(lg * (c - 1.0 - ri))).astype(BF16)
        state_ref[h] = math.exp(lg * c) * state + _dot_tn(k_dec, v_bf)
        o = inner + cross
        mu = jnp.mean(o, axis=-1, keepdims=True)
        d = o - mu
        var = jnp.mean(d * d, axis=-1, keepdims=True)
        on = d * lax.rsqrt(var + GN_EPS) * ng_ref[:, sl] + nb_ref[:, sl]
        g = g_ref[:, sl]
        o_ref[:, sl] = (g * jax.nn.sigmoid(g) * on).astype(BF16)


def _retention(ret, cos_t, sin_t, ng, nb, layer):
    t = ret.shape[0]
    c = RET_CHUNK
    col = lambda j: pl.BlockSpec((c, W_BR), lambda i, j=j: (i, j))
    vec = pl.BlockSpec((None, 1, W_BR), lambda i: (layer, 0, 0))
    tab = pl.BlockSpec((c, RET_DH), lambda i: (i, 0))
    return pl.pallas_call(
        _ret_kernel,
        grid=(t // c,),
        in_specs=[col(0), col(1), col(2), col(3), tab, tab, vec, vec],
        out_specs=pl.BlockSpec((c, W_BR), lambda i: (i, 0)),
        out_shape=jax.ShapeDtypeStruct((t, W_BR), BF16),
        scratch_shapes=[pltpu.VMEM((RET_HEADS, RET_DH, RET_DH), F32)],
        compiler_params=_params(("arbitrary",), 32),
        name="retention",
    )(ret, ret, ret, ret, cos_t, sin_t, ng, nb)


def _seg64_sum(x):
    tile = 2 * RWKV_DH
    first = lax.broadcasted_iota(jnp.int32, (x.shape[0], tile), 1) < RWKV_DH
    outs = []
    for t in range(x.shape[1] // tile):
        xt = x[:, t * tile:(t + 1) * tile]
        s0 = jnp.sum(jnp.where(first, xt, 0.0), axis=-1, keepdims=True)
        s1 = jnp.sum(jnp.where(first, 0.0, xt), axis=-1, keepdims=True)
        outs.append(jnp.where(first, s0, s1))
    return jnp.concatenate(outs, axis=1)


def _cumsum_rows(x):
    rows = lax.broadcasted_iota(jnp.int32, x.shape, 0)
    s = 1
    while s < x.shape[0]:
        x = x + jnp.where(rows >= s, pltpu.roll(x, s, 0), 0.0)
        s *= 2
    return x


def _rwkv_kernel(z_ref, mu_ref, w0_ref, a0_ref, kk_ref, ka_ref, rk_ref, ng_ref, nb_ref,
                 w2_ref, a2_ref, g2_ref, o_ref, last_ref, s_ref):
    L = RWKV_CHUNK
    P = 2 * RWKV_DH

    @pl.when(pl.program_id(0) == 0)
    def _():
        last_ref[...] = jnp.zeros_like(last_ref)
        s_ref[...] = jnp.zeros_like(s_ref)

    z = z_ref[...]
    rows = lax.broadcasted_iota(jnp.int32, z.shape, 0)
    zprev = jnp.where(rows == 0, last_ref[...], pltpu.roll(z, 1, 0))
    last_ref[...] = z[z.shape[0] - 1:, :]
    zs = z + mu_ref[...] * (zprev - z)
    r = zs[:, 0:W_BR]
    k = zs[:, W_BR:2 * W_BR]
    v = zs[:, 2 * W_BR:3 * W_BR]
    w_lo = zs[:, 3 * W_BR:3 * W_BR + 64]
    a_lo = zs[:, 3 * W_BR + 64:3 * W_BR + 128]
    g_lo = zs[:, 3 * W_BR + 128:3 * W_BR + 256]

    wpre = w0_ref[...] + _dot(jnp.tanh(w_lo).astype(BF16), w2_ref[...])
    nw = -wpre
    softplus = jnp.maximum(nw, 0.0) + jnp.log1p(jnp.exp(-jnp.abs(nw)))
    w = -softplus - 0.5
    lw = -jnp.exp(w)
    a = jax.nn.sigmoid(a0_ref[...] + _dot(a_lo.astype(BF16), a2_ref[...]))
    g = _dot(jax.nn.sigmoid(g_lo).astype(BF16), g2_ref[...])

    kkv = k * kk_ref[...]
    kkn = kkv / jnp.maximum(jnp.sqrt(_seg64_sum(kkv * kkv)), 1e-12)
    k2 = k * (1.0 + (a - 1.0) * ka_ref[...])
    b = kkn * a

    nchunk = z.shape[0] // L
    chunks = range(nchunk)
    rows_of = lambda x, c: x[c * L:(c + 1) * L, :]
    cum = jnp.concatenate([_cumsum_rows(rows_of(lw, c)) for c in chunks], axis=0)
    cum_last = jnp.concatenate(
        [jnp.broadcast_to(cum[(c + 1) * L - 1:(c + 1) * L, :], (L, W_BR)) for c in chunks], axis=0)
    e_to_end = jnp.exp(cum_last - cum)
    e_inv = jnp.exp(-cum)
    kkd = kkn * jnp.exp(cum - lw)
    rd = r * jnp.exp(cum)
    bd = b * e_inv
    kd = k2 * e_inv
    bd2 = b * e_to_end
    kd2 = k2 * e_to_end

    li = lax.broadcasted_iota(jnp.int32, (P, P), 0)
    lj = lax.broadcasted_iota(jnp.int32, (P, P), 1)
    tt = li % L
    ts = lj % L
    strict = tt > ts
    incl = tt >= ts
    eye = jnp.where(li == lj, 1.0, 0.0)
    first_half = lax.broadcasted_iota(jnp.int32, (L, P), 1) < RWKV_DH

    def stack(x, c, p):
        xp = x[c * L:(c + 1) * L, p * P:(p + 1) * P]
        return jnp.concatenate([jnp.where(first_half, xp, 0.0), jnp.where(first_half, 0.0, xp)],
                               axis=0).astype(BF16)

    pairs = range(RWKV_HEADS // 2)
    cps = [(c, p) for c in chunks for p in pairs]
    kkm = {cp: stack(kkd, *cp) for cp in cps}
    rm = {cp: stack(rd, *cp) for cp in cps}
    vm = {cp: stack(v, *cp) for cp in cps}
    gram = {cp: _dot_nt(jnp.concatenate([kkm[cp], rm[cp]], axis=0),
                        jnp.concatenate([stack(bd, *cp), stack(kd, *cp)], axis=0)) for cp in cps}
    a_bb = {cp: jnp.where(strict, gram[cp][:P, :P], 0.0) for cp in cps}
    av = {cp: _dot(jnp.where(strict, gram[cp][:P, P:], 0.0).astype(BF16), vm[cp]) for cp in cps}
    r_bk = {cp: jnp.concatenate([jnp.where(incl, gram[cp][P:, :P], 0.0), jnp.where(incl, gram[cp][P:, P:], 0.0)],
                                axis=1).astype(BF16) for cp in cps}
    lvl = (tt // 2) == (ts // 2)
    t_inv = {cp: eye - jnp.where(lvl, a_bb[cp], 0.0) for cp in cps}
    blk = 2
    while blk < L:
        lvl = ((tt // (2 * blk)) == (ts // (2 * blk))) & ((tt // blk) != (ts // blk))
        t_bf = {cp: t_inv[cp].astype(BF16) for cp in cps}
        mt = {cp: _dot(jnp.where(lvl, a_bb[cp], 0.0).astype(BF16), t_bf[cp]).astype(BF16) for cp in cps}
        t_inv = {cp: t_inv[cp] - _dot(t_bf[cp], mt[cp]) for cp in cps}
        blk *= 2
    t_bf = {cp: t_inv[cp].astype(BF16) for cp in cps}

    state = [s_ref[p] for p in pairs]
    y_rows = []
    for c in chunks:
        s_bf = [state[p].astype(BF16) for p in pairs]
        wm = [(_dot_nt(kkm[c, p], s_bf[p]) + av[c, p]).astype(BF16) for p in pairs]
        ys0 = [_dot_nt(rm[c, p], s_bf[p]) for p in pairs]
        bkm2 = [jnp.concatenate([stack(bd2, c, p), stack(kd2, c, p)], axis=0) for p in pairs]
        p_last = jnp.exp(cum[(c + 1) * L - 1:(c + 1) * L, :])
        uv = [jnp.concatenate([(-_dot(t_bf[c, p], wm[p])).astype(BF16), vm[c, p]], axis=0)
              for p in pairs]
        state = [state[p] * p_last[:, p * P:(p + 1) * P] + _dot_tn(uv[p], bkm2[p]) for p in pairs]
        ys = [ys0[p] + _dot(r_bk[c, p], uv[p]) for p in pairs]
        y_rows.append(jnp.concatenate([y[:L] + y[L:] for y in ys], axis=1))
    for p in pairs:
        s_ref[p] = state[p]
    y = jnp.concatenate(y_rows, axis=0)

    inv_n = 1.0 / RWKV_DH
    mu = _seg64_sum(y) * inv_n
    d = y - mu
    var = _seg64_sum(d * d) * inv_n
    yn = d * lax.rsqrt(var + RWKV_GN_EPS) * ng_ref[...] + nb_ref[...]
    bonus = _seg64_sum(r * k2 * rk_ref[...]) * v
    o_ref[...] = ((yn + bonus) * g).astype(BF16)


def _rwkv(rz, prm, layer):
    t = rz.shape[0]
    tr = min(RWKV_ROWS, t)
    vec = lambda n: pl.BlockSpec((None, 1, n), lambda i: (layer, 0, 0))
    mat = lambda r: pl.BlockSpec((None, r, W_BR), lambda i: (layer, 0, 0))
    return pl.pallas_call(
        _rwkv_kernel,
        grid=(t // tr,),
        in_specs=[pl.BlockSpec((tr, RWKV_IN), lambda i: (i, 0)), vec(RWKV_IN)] + [vec(W_BR)] * 7
                 + [mat(64), mat(64), mat(128)],
        out_specs=pl.BlockSpec((tr, W_BR), lambda i: (i, 0)),
        out_shape=jax.ShapeDtypeStruct((t, W_BR), BF16),
        scratch_shapes=[pltpu.VMEM((1, RWKV_IN), F32),
                        pltpu.VMEM((RWKV_HEADS // 2, 2 * RWKV_DH, 2 * RWKV_DH), F32)],
        compiler_params=_params(("arbitrary",), 32),
        name="rwkv7",
    )(rz, prm["mu"], prm["w0"], prm["a0"], prm["k_k"], prm["k_a"], prm["r_k"], prm["ng"], prm["nb"],
      prm["w2"], prm["a2"], prm["g2"])


def _gelu_tanh(x):
    c = math.sqrt(2.0 / math.pi)
    return x * (0.5 * (1.0 + jnp.tanh(c * (x + 0.044715 * (x * x * x)))))


def _s5_kernel(*refs):
    C = SSM_CHUNK
    u_refs = refs[:C]
    (lre_ref, lim_ref, ls_ref, bcat_ref, bswp_ref, ccat_ref, cswp_ref, d_ref, z_ref,
     et_ref, bb_ref, fbig_ref, wot_ref, mbig_ref, x_ref, prev_ref, a_ref) = refs[C:]
    GL = SSM_GROUP * SSM_GB
    PS = SSM_STATE
    half = SSM_GB * PS
    n = u_refs[0].shape[0]

    @pl.when(pl.program_id(0) == 0)
    def _():
        fbig_ref[...] = jnp.zeros_like(fbig_ref)
        wot_ref[...] = jnp.zeros_like(wot_ref)
        mbig_ref[...] = jnp.zeros_like(mbig_ref)

    first1 = lax.broadcasted_iota(jnp.int32, (1, GL), 1) < PS
    sgn = jnp.where(first1, -1.0, 1.0)
    first16 = lax.broadcasted_iota(jnp.int32, (SSM_GROUP, GL), 1) < PS
    tau = lax.broadcasted_iota(jnp.int32, (24, GL), 0).astype(F32)

    for g in range(SSM_GB):
        lr = lre_ref[g:g + 1, :]
        li = lim_ref[g:g + 1, :]
        dt = jnp.exp(ls_ref[g:g + 1, :])
        mag = jnp.exp(lr * dt)
        ab_re = mag * jnp.cos(li * dt)
        ab_im = mag * jnp.sin(li * dt)
        denom = lr * lr + li * li
        f_re = ((ab_re - 1.0) * lr + ab_im * li) / denom
        f_im = (ab_im * lr - (ab_re - 1.0) * li) / denom
        bcat = bcat_ref[g]
        bswp = bswp_ref[g]
        bbs = f_re * bcat + (sgn * f_im) * bswp
        bbw = f_re * bswp - (sgn * f_im) * bcat
        pmag = jnp.exp(tau * (lr * dt))
        pw_re = pmag * jnp.cos(tau * (li * dt))
        pw_im = pmag * jnp.sin(tau * (li * dt))
        ccat = ccat_ref[g]
        cswp = cswp_ref[g]
        rows = slice(g * SSM_GROUP, (g + 1) * SSM_GROUP)
        tile = slice((g // 2) * GL, (g // 2 + 1) * GL)
        tile_im = slice(half + (g // 2) * GL, half + (g // 2 + 1) * GL)
        odd = g % 2

        def planes(xy, yx):
            if odd:
                return jnp.where(first16, 0.0, yx).astype(BF16), jnp.where(first16, 0.0, xy).astype(BF16)
            return jnp.where(first16, xy, 0.0).astype(BF16), jnp.where(first16, yx, 0.0).astype(BF16)

        for t in range(C + 1):
            pr = pw_re[t:t + 1, :]
            pi = pw_im[t:t + 1, :]
            e = pr * (-sgn * ccat) - pi * cswp
            et_ref[t, rows, :] = e
            if t >= 1:
                esw = pr * (sgn * cswp) - pi * ccat
                o_re, o_im = planes(e, esw)
                orow = slice((t - 1) * GL + g * SSM_GROUP, (t - 1) * GL + (g + 1) * SSM_GROUP)
                wot_ref[orow, tile] = o_re
                wot_ref[orow, tile_im] = o_im
            if t < C:
                qr = pw_re[C - 1 - t:C - t, :]
                qi = pw_im[C - 1 - t:C - t, :]
                f = qr * bbs + qi * (sgn * bbw)
                fsw = qr * bbw - qi * (sgn * bbs)
                i_re, i_im = planes(f, fsw)
                irow = slice(t * GL + g * SSM_GROUP, t * GL + (g + 1) * SSM_GROUP)
                fbig_ref[irow, tile] = i_re
                fbig_ref[irow, tile_im] = i_im
        bb_ref[rows, :] = bbs
        lanes = slice(odd * PS, (odd + 1) * PS)
        dst = slice((g // 2) * GL + odd * PS, (g // 2) * GL + (odd + 1) * PS)
        a_ref[0:1, dst] = pw_re[C:C + 1, lanes]
        a_ref[1:2, dst] = pw_im[C:C + 1, lanes]

    bi = lax.broadcasted_iota(jnp.int32, (GL, GL), 0) // SSM_GROUP
    bj = lax.broadcasted_iota(jnp.int32, (GL, GL), 1) // SSM_GROUP
    bh, bl = _split2(bb_ref[...])
    for t in range(C):
        eh, el = _split2(et_ref[t])
        kt = jnp.where(bi == bj, _dot_nt(bh, eh) + _dot_nt(bh, el) + _dot_nt(bl, eh), 0.0).astype(BF16)
        for j in range(C - t):
            mbig_ref[j * GL:(j + 1) * GL, (j + t) * GL:(j + t + 1) * GL] = kt

    ucat = jnp.concatenate([u_refs[j][...].astype(BF16) for j in range(C)], axis=1)
    x_ref[...] = _dot(ucat, fbig_ref[...])

    a_re = a_ref[0:1, :]
    a_im = a_ref[1:2, :]

    def step(i, carry):
        s_re, s_im = carry
        prev_ref[pl.ds(i, 1), 0:half] = s_re
        prev_ref[pl.ds(i, 1), half:2 * half] = s_im
        x_re = x_ref[pl.ds(i, 1), 0:half]
        x_im = x_ref[pl.ds(i, 1), half:2 * half]
        return (a_re * s_re - a_im * s_im + x_re, a_re * s_im + a_im * s_re + x_im)

    zero = jnp.zeros((1, half), F32)
    lax.fori_loop(0, n, step, (zero, zero), unroll=8)

    prev_bf = prev_ref[...].astype(BF16)
    steps_per_dot = 4
    for q in range(C // steps_per_dot):
        kq = (q + 1) * steps_per_dot * GL
        cols = slice(q * steps_per_dot * GL, (q + 1) * steps_per_dot * GL)
        y = _dot(ucat[:, :kq], mbig_ref[0:kq, cols]) + _dot_nt(prev_bf, wot_ref[cols, :])
        for ii in range(steps_per_dot):
            i = q * steps_per_dot + ii
            yi = y[:, ii * GL:(ii + 1) * GL] + d_ref[...] * u_refs[i][...]
            z_ref[i] = _gelu_tanh(yi).astype(BF16)


def _s5(u, prm, layer):
    t = u.shape[0]
    C = SSM_CHUNK
    n = t // C
    gl = SSM_GROUP * SSM_GB
    nblk = W_BR // gl
    st = 2 * SSM_GB * SSM_STATE
    u2 = u.reshape(n, C * W_BR)
    u_specs = [pl.BlockSpec((n, gl), lambda b, j=j: (0, j * nblk + b)) for j in range(C)]
    lam = pl.BlockSpec((None, SSM_GB, gl), lambda b: (layer, b, 0))
    bc = pl.BlockSpec((None, SSM_GB, SSM_GROUP, gl), lambda b: (layer, b, 0, 0))
    return pl.pallas_call(
        _s5_kernel,
        grid=(nblk,),
        in_specs=u_specs + [lam, lam, lam, bc, bc, bc, bc, pl.BlockSpec((None, 1, gl), lambda b: (layer, 0, b))],
        out_specs=pl.BlockSpec((C, n, gl), lambda b: (0, 0, b)),
        out_shape=jax.ShapeDtypeStruct((C, n, W_BR), BF16),
        scratch_shapes=[pltpu.VMEM((C + 1, gl, gl), F32), pltpu.VMEM((gl, gl), F32),
                        pltpu.VMEM((C * gl, st), BF16), pltpu.VMEM((C * gl, st), BF16),
                        pltpu.VMEM((C * gl, C * gl), BF16),
                        pltpu.VMEM((n, st), F32), pltpu.VMEM((n, st), F32), pltpu.VMEM((8, st // 2), F32)],
        compiler_params=_params(("arbitrary",), 56),
        name="s5",
    )(*([u2] * C), prm["lam_re"], prm["lam_im"], prm["log_step"], prm["b_cat"], prm["b_swp"], prm["c_cat"],
      prm["c_swp"], prm["d"])


def _merge_kernel(z_ref, r_ref, k_ref, wa_ref, wb_ref, wr_ref, wk_ref, g0_ref, g1_ref, g2_ref, o_ref):
    z = z_ref[...]
    y_ssm = _dot(z, wa_ref[...]) * jax.nn.sigmoid(_dot(z, wb_ref[...]))
    y_ret = _dot(r_ref[...], wr_ref[...])
    y_rwkv = _dot(k_ref[...], wk_ref[...])
    o_ref[...] = (g0_ref[...] * y_ssm + g1_ref[...] * y_ret + g2_ref[...] * y_rwkv).astype(BF16)


def _merge(z, o_ret, o_rwkv, w_glu, w_ret, w_rwkv, gates, layer):
    t = z.shape[0]
    tm = min(1024, t)
    tn = 512
    nb = D_MODEL // tn
    act = pl.BlockSpec((tm, W_BR), lambda i, j: (i, 0))
    wsp = lambda off: pl.BlockSpec((None, W_BR, tn), lambda i, j, off=off: (layer, 0, j + off))
    gsp = lambda off: pl.BlockSpec((tm, tn), lambda i, j, off=off: (i, j + off))
    return pl.pallas_call(
        _merge_kernel,
        grid=(t // tm, nb),
        in_specs=[act, act, act, wsp(0), wsp(nb), wsp(0), wsp(0), gsp(0), gsp(nb), gsp(2 * nb)],
        out_specs=pl.BlockSpec((tm, tn), lambda i, j: (i, j)),
        out_shape=jax.ShapeDtypeStruct((t, D_MODEL), BF16),
        compiler_params=_params(("parallel", "arbitrary"), 48),
        name="merge",
    )(z, o_ret, o_rwkv, w_glu, w_glu, w_ret, w_rwkv, gates, gates, gates)


def _out_ln_kernel(m_ref, w_ref, x_ref, g_ref, b_ref, o_ref, obf_ref):
    y = DEEPNORM_ALPHA * x_ref[...] + _dot(m_ref[...], w_ref[...])
    out = _layer_norm(y, g_ref[...], b_ref[...])
    o_ref[...] = out
    obf_ref[...] = out.astype(BF16)


def _out_ln(m, w_all, x, g, b, layer):
    t = x.shape[0]
    tm = min(256, t)
    row = pl.BlockSpec((tm, D_MODEL), lambda i: (i, 0))
    vec = pl.BlockSpec((None, 1, D_MODEL), lambda i: (layer, 0, 0))
    return pl.pallas_call(
        _out_ln_kernel,
        grid=(t // tm,),
        in_specs=[row, pl.BlockSpec((None, D_MODEL, D_MODEL), lambda i: (layer, 0, 0)), row, vec, vec],
        out_specs=[row, row],
        out_shape=[jax.ShapeDtypeStruct((t, D_MODEL), F32), jax.ShapeDtypeStruct((t, D_MODEL), BF16)],
        compiler_params=_params(("parallel",), 48),
        name="out_ln",
    )(m, w_all, x, g, b)


def _ffn_up_kernel(x_ref, w1_ref, w2_ref, c1_ref, c2_ref, o_ref, h1_ref, h2_ref):
    tm = x_ref.shape[0]

    @pl.when(pl.program_id(1) == 0)
    def _():
        h1_ref[0:8, :] = jnp.zeros((8, h1_ref.shape[1]), F32)
        h2_ref[0:8, :] = jnp.zeros((8, h2_ref.shape[1]), F32)

    x = x_ref[...]

    def conv(w_ref, c_ref, h_ref):
        h = _dot(x, w_ref[...])
        h_ref[8:8 + tm, :] = h
        out = c_ref[2:3, :] * h + c_ref[1:2, :] * h_ref[7:7 + tm, :] + c_ref[0:1, :] * h_ref[6:6 + tm, :]
        h_ref[0:8, :] = h[tm - 8:tm, :]
        return out

    a = conv(w1_ref, c1_ref, h1_ref)
    b = conv(w2_ref, c2_ref, h2_ref)
    o_ref[...] = (a * jax.nn.sigmoid(a) * b).astype(BF16)


def _ffn_up(x_bf, w_up, w_conv, layer):
    t = x_bf.shape[0]
    tm = min(1024, t)
    tn = 512
    nb = D_FF // tn
    wsp = lambda off: pl.BlockSpec((None, D_MODEL, tn), lambda j, i, off=off: (layer, 0, j + off))
    csp = lambda off: pl.BlockSpec((None, 3, tn), lambda j, i, off=off: (layer, 0, j + off))
    return pl.pallas_call(
        _ffn_up_kernel,
        grid=(nb, t // tm),
        in_specs=[pl.BlockSpec((tm, D_MODEL), lambda j, i: (i, 0)), wsp(0), wsp(nb), csp(0), csp(nb)],
        out_specs=pl.BlockSpec((tm, tn), lambda j, i: (i, j)),
        out_shape=jax.ShapeDtypeStruct((t, D_FF), BF16),
        scratch_shapes=[pltpu.VMEM((tm + 8, tn), F32), pltpu.VMEM((tm + 8, tn), F32)],
        compiler_params=_params(("parallel", "arbitrary"), 48),
        name="ffn_up",
    )(x_bf, w_up, w_up, w_conv, w_conv)


def _ffn_down_kernel(a_ref, w_ref, x_ref, g_ref, b_ref, o_ref, obf_ref, acc_ref):
    kk = pl.program_id(1)

    @pl.when(kk == 0)
    def _():
        acc_ref[...] = DEEPNORM_ALPHA * x_ref[...]

    acc_ref[...] += _dot(a_ref[...], w_ref[...])

    @pl.when(kk == pl.num_programs(1) - 1)
    def _():
        out = _layer_norm(acc_ref[...], g_ref[...], b_ref[...])
        o_ref[...] = out
        obf_ref[...] = out.astype(BF16)


def _ffn_down(act, w_all, x, g, b, layer):
    t = x.shape[0]
    tm = min(512, t)
    tk = 512
    row = pl.BlockSpec((tm, D_MODEL), lambda i, k: (i, 0))
    vec = pl.BlockSpec((None, 1, D_MODEL), lambda i, k: (layer, 0, 0))
    return pl.pallas_call(
        _ffn_down_kernel,
        grid=(t // tm, D_FF // tk),
        in_specs=[pl.BlockSpec((tm, tk), lambda i, k: (i, k)),
                  pl.BlockSpec((None, tk, D_MODEL), lambda i, k: (layer, k, 0)), row, vec, vec],
        out_specs=[row, row],
        out_shape=[jax.ShapeDtypeStruct((t, D_MODEL), F32), jax.ShapeDtypeStruct((t, D_MODEL), BF16)],
        scratch_shapes=[pltpu.VMEM((tm, D_MODEL), F32)],
        compiler_params=_params(("parallel", "arbitrary"), 48),
        name="ffn_down",
    )(act, w_all, x, g, b)


def kernel(x, positions, w_in, ssm_lambda_re, ssm_lambda_im, ssm_log_step, ssm_b_re, ssm_b_im, ssm_c_re, ssm_c_im, ssm_d, ssm_glu, ret_norm_g, ret_norm_b, ret_out, rwkv_mu, rwkv_w0, rwkv_w2, rwkv_a0, rwkv_a2, rwkv_g2, rwkv_k_k, rwkv_k_a, rwkv_r_k, rwkv_norm_g, rwkv_norm_b, rwkv_out, w_o, ln1_g, ln1_b, ffn_up, ffn_conv, ffn_down, ln2_g, ln2_b):
    bsz, t, _ = x.shape
    assert bsz == 1
    depth = w_in.shape[0]
    nchunk = t // SSM_CHUNK
    vec3 = lambda p: p.reshape(depth, 1, p.shape[-1])

    w_in_bf = w_in.astype(BF16)
    glu_bf = ssm_glu.astype(BF16)
    ret_out_bf = ret_out.astype(BF16)
    rwkv_out_bf = rwkv_out.astype(BF16)
    w_o_bf = w_o.astype(BF16)
    up_bf = ffn_up.astype(BF16)
    down_bf = ffn_down.astype(BF16)

    twice = lambda a, b: jnp.concatenate([a, b], axis=-1)
    b_re_t, b_im_t = jnp.swapaxes(ssm_b_re, -1, -2), jnp.swapaxes(ssm_b_im, -1, -2)
    log_step = jnp.broadcast_to(ssm_log_step[..., None], ssm_lambda_re.shape)
    s5_prm = dict(
        lam_re=twice(ssm_lambda_re, ssm_lambda_re), lam_im=twice(ssm_lambda_im, ssm_lambda_im),
        log_step=twice(log_step, log_step),
        b_cat=twice(b_re_t, b_im_t), b_swp=twice(b_im_t, b_re_t),
        c_cat=twice(ssm_c_re, ssm_c_im), c_swp=twice(ssm_c_im, ssm_c_re),
        d=vec3(ssm_d))
    rwkv_prm = dict(mu=vec3(rwkv_mu), w0=vec3(rwkv_w0), a0=vec3(rwkv_a0), k_k=vec3(rwkv_k_k), k_a=vec3(rwkv_k_a),
                    r_k=vec3(rwkv_r_k), ng=vec3(rwkv_norm_g), nb=vec3(rwkv_norm_b),
                    w2=rwkv_w2.astype(BF16), a2=rwkv_a2.astype(BF16), g2=rwkv_g2.astype(BF16))
    ret_g, ret_b = vec3(ret_norm_g), vec3(ret_norm_b)
    ln1g, ln1b, ln2g, ln2b = vec3(ln1_g), vec3(ln1_b), vec3(ln2_g), vec3(ln2_b)

    cos_t, sin_t = _rope_tables(positions)
    xf = x.reshape(t, D_MODEL)
    xb = xf.astype(BF16)
    for l in range(depth):
        u = _proj(xb, w_in_bf, l, 0, W_BR, 512)
        ret = _proj(xb, w_in_bf, l, OFF_RET, 4 * W_BR, 1024)
        rz = _proj(xb, w_in_bf, l, OFF_RWKV, RWKV_IN, 256)
        gates = _proj(xb, w_in_bf, l, OFF_GATE, N_GATE, 768, act="sigmoid")
        z = _s5(u, s5_prm, l).transpose(1, 0, 2).reshape(t, W_BR)
        o_ret = _retention(ret, cos_t, sin_t, ret_g, ret_b, l)
        o_rwkv = _rwkv(rz, rwkv_prm, l)
        merged = _merge(z, o_ret, o_rwkv, glu_bf, ret_out_bf, rwkv_out_bf, gates, l)
        xf, xb = _out_ln(merged, w_o_bf, xf, ln1g, ln1b, l)
        act = _ffn_up(xb, up_bf, ffn_conv, l)
        xf, xb = _ffn_down(act, down_bf, xf, ln2g, ln2b, l)
    return xf.reshape(bsz, t, D_MODEL)
```

```python
import functools
import math

import jax
import jax.numpy as jnp
from jax import lax
from jax.experimental import pallas as pl
from jax.experimental.pallas import tpu as pltpu

F32 = jnp.float32
BF16 = jnp.bfloat16

D_MODEL = 2048
DEPTH = 4
W_BR = 1024
SSM_GROUP = 16
SSM_GROUPS = 64
SSM_STATE = 64
SSM_CHUNK = 16
SSM_GB = 8
RET_HEADS = 8
RET_DH = 128
RET_CHUNK = 128
ROPE_BASE = 10000.0
RWKV_HEADS = 16
RWKV_DH = 64
RWKV_CHUNK = 64
RWKV_ROWS = 256
RWKV_IN = 3328
D_FF = 5632
OFF_RET = 1024
OFF_RWKV = 5120
OFF_GATE = 8448
N_GATE = 3 * D_MODEL
DEEPNORM_ALPHA = (2.0 * DEPTH) ** 0.25
LN_EPS = 1e-5
GN_EPS = 1e-5
RWKV_GN_EPS = 64e-5
MIB = 1024 * 1024
SUB_ROWS = 256


def _params(sem, vmem_mib):
    return pltpu.CompilerParams(dimension_semantics=sem, vmem_limit_bytes=vmem_mib * MIB)


def _dot(a, b):
    return jnp.dot(a, b, preferred_element_type=F32)


def _dot_nt(a, b):
    return lax.dot_general(a, b, (((1,), (1,)), ((), ())), preferred_element_type=F32)


def _dot_tn(a, b):
    return lax.dot_general(a, b, (((0,), (0,)), ((), ())), preferred_element_type=F32)


def _split2(x):
    hi = x.astype(BF16)
    lo = (x - hi.astype(F32)).astype(BF16)
    return hi, lo


def _split3(x):
    hi = x.astype(BF16)
    r1 = x - hi.astype(F32)
    mid = r1.astype(BF16)
    lo = (r1 - mid.astype(F32)).astype(BF16)
    return hi, mid, lo


def _layer_norm(y, g, b):
    mu = jnp.mean(y, axis=-1, keepdims=True)
    d = y - mu
    var = jnp.mean(d * d, axis=-1, keepdims=True)
    return d * lax.rsqrt(var + LN_EPS) * g + b


def _cast_once(w_ref, wbf_ref):
    @pl.when(pl.program_id(1) == 0)
    def _():
        wbf_ref[...] = w_ref[...].astype(BF16)


def _proj_kernel(x_ref, w_ref, o_ref, wbf_ref, *, act):
    _cast_once(w_ref, wbf_ref)
    acc = _dot(x_ref[...], wbf_ref[...])
    if act == "sigmoid":
        acc = jax.nn.sigmoid(acc)
    o_ref[...] = acc.astype(o_ref.dtype)


def _proj(x_bf, w_all, layer, col0, ncols, tn, act=None):
    t, k = x_bf.shape
    tm = min(1024, t)
    off = col0 // tn
    assert off * tn == col0 and ncols % tn == 0
    return pl.pallas_call(
        functools.partial(_proj_kernel, act=act),
        grid=(ncols // tn, t // tm),
        in_specs=[pl.BlockSpec((tm, k), lambda j, i: (i, 0)),
                  pl.BlockSpec((None, k, tn), lambda j, i: (layer, 0, j + off))],
        out_specs=pl.BlockSpec((tm, tn), lambda j, i: (i, j)),
        out_shape=jax.ShapeDtypeStruct((t, ncols), F32),
        scratch_shapes=[pltpu.VMEM((k, tn), BF16)],
        compiler_params=_params(("parallel", "arbitrary"), 48),
        name="proj",
    )(x_bf, w_all)


def _rope_kernel(pos_ref, cos_ref, sin_ref):
    half = RET_DH // 2
    pos = pos_ref[...].astype(F32)
    lane = lax.broadcasted_iota(jnp.int32, (1, RET_DH), 1)
    idx = jnp.where(lane < half, lane, lane - half).astype(F32)
    inv_freq = jnp.exp(-(idx / half) * math.log(ROPE_BASE))
    ang = pos * inv_freq
    cos_ref[...] = jnp.cos(ang)
    s = jnp.sin(ang)
    sin_ref[...] = jnp.where(lane < half, -s, s)


def _rope_tables(positions):
    t = positions.shape[-1]
    tm = min(1024, t)
    pos = positions.reshape(t, 1)
    return pl.pallas_call(
        _rope_kernel,
        grid=(t // tm,),
        in_specs=[pl.BlockSpec((tm, 1), lambda i: (i, 0))],
        out_specs=[pl.BlockSpec((tm, RET_DH), lambda i: (i, 0))] * 2,
        out_shape=[jax.ShapeDtypeStruct((t, RET_DH), F32)] * 2,
        compiler_params=_params(("parallel",), 16),
        name="rope",
    )(pos)


_RET_LOG_GAMMA = [math.log1p(-(2.0 ** (-5.0 - h))) for h in range(RET_HEADS)]


def _ret_kernel(q_ref, k_ref, v_ref, g_ref, cos_ref, sin_ref, ng_ref, nb_ref, o_ref, state_ref):
    c = RET_CHUNK

    @pl.when(pl.program_id(0) == 0)
    def _():
        state_ref[...] = jnp.zeros_like(state_ref)

    cos = cos_ref[...]
    sin = sin_ref[...]
    ri = lax.broadcasted_iota(jnp.int32, (c, c), 0).astype(F32)
    ci = lax.broadcasted_iota(jnp.int32, (c, c), 1).astype(F32)
    rel = ri - ci
    scale = RET_DH ** -0.5
    for h in range(RET_HEADS):
        sl = slice(h * RET_DH, (h + 1) * RET_DH)
        lg = _RET_LOG_GAMMA[h]
        q = q_ref[:, sl]
        k = k_ref[:, sl]
        qr = q * cos + pltpu.roll(q, RET_DH // 2, 1) * sin
        kr = (k * cos + pltpu.roll(k, RET_DH // 2, 1) * sin) * scale
        v_bf = v_ref[:, sl].astype(BF16)
        decay = jnp.where(rel >= 0, jnp.exp(lg * jnp.maximum(rel, 0.0)), 0.0)
        q_bf = qr.astype(BF16)
        scores = _dot_nt(q_bf, kr.astype(BF16)) * decay
        inner = _dot(scores.astype(BF16), v_bf)
        state = state_ref[h]
        cross = _dot(q_bf, state.astype(BF16)) * jnp.exp(lg * (ri + 1.0))
        k_dec = (kr * jnp.exp(lg * (c - 1.0 - ri))).astype(BF16)
        state_ref[h] = math.exp(lg * c) * state + _dot_tn(k_dec, v_bf)
        o = inner + cross
        mu = jnp.mean(o, axis=-1, keepdims=True)
        d = o - mu
        var = jnp.mean(d * d, axis=-1, keepdims=True)
        on = d * lax.rsqrt(var + GN_EPS) * ng_ref[:, sl] + nb_ref[:, sl]
        g = g_ref[:, sl]
        o_ref[:, sl] = (g * jax.nn.sigmoid(g) * on).astype(BF16)


def _retention(proj, cos_t, sin_t, ng, nb, layer):
    ret = proj
    t = ret.shape[0]
    c = RET_CHUNK
    col = lambda j: pl.BlockSpec((c, W_BR), lambda i, j=j: (i, j + OFF_RET // W_BR))
    vec = pl.BlockSpec((None, 1, W_BR), lambda i: (layer, 0, 0))
    tab = pl.BlockSpec((c, RET_DH), lambda i: (i, 0))
    return pl.pallas_call(
        _ret_kernel,
        grid=(t // c,),
        in_specs=[col(0), col(1), col(2), col(3), tab, tab, vec, vec],
        out_specs=pl.BlockSpec((c, W_BR), lambda i: (i, 0)),
        out_shape=jax.ShapeDtypeStruct((t, W_BR), BF16),
        scratch_shapes=[pltpu.VMEM((RET_HEADS, RET_DH, RET_DH), F32)],
        compiler_params=_params(("arbitrary",), 32),
        name="retention",
    )(ret, ret, ret, ret, cos_t, sin_t, ng, nb)


def _seg64_sum(x):
    tile = 2 * RWKV_DH
    first = lax.broadcasted_iota(jnp.int32, (x.shape[0], tile), 1) < RWKV_DH
    outs = []
    for t in range(x.shape[1] // tile):
        xt = x[:, t * tile:(t + 1) * tile]
        s0 = jnp.sum(jnp.where(first, xt, 0.0), axis=-1, keepdims=True)
        s1 = jnp.sum(jnp.where(first, 0.0, xt), axis=-1, keepdims=True)
        outs.append(jnp.where(first, s0, s1))
    return jnp.concatenate(outs, axis=1)


def _cumsum_rows(x):
    rows = lax.broadcasted_iota(jnp.int32, x.shape, 0)
    s = 1
    while s < x.shape[0]:
        x = x + jnp.where(rows >= s, pltpu.roll(x, s, 0), 0.0)
        s *= 2
    return x


def _rwkv_kernel(r_ref, k_ref, v_ref, lo_ref, mu_ref, w0_ref, a0_ref, kk_ref, ka_ref, rk_ref, ng_ref, nb_ref,
                 w2_ref, a2_ref, g2_ref, o_ref, last_ref, s_ref):
    L = RWKV_CHUNK
    P = 2 * RWKV_DH
    nrows = r_ref.shape[0]

    @pl.when(pl.program_id(0) == 0)
    def _():
        last_ref[...] = jnp.zeros_like(last_ref)
        s_ref[...] = jnp.zeros_like(s_ref)

    def token_shift(z_ref, c0, c1):
        z = z_ref[...]
        rows = lax.broadcasted_iota(jnp.int32, z.shape, 0)
        zprev = jnp.where(rows == 0, last_ref[:, c0:c1], pltpu.roll(z, 1, 0))
        last_ref[:, c0:c1] = z[nrows - 1:, :]
        return z + mu_ref[:, c0:c1] * (zprev - z)

    r = token_shift(r_ref, 0, W_BR)
    k = token_shift(k_ref, W_BR, 2 * W_BR)
    v = token_shift(v_ref, 2 * W_BR, 3 * W_BR)
    lo = token_shift(lo_ref, 3 * W_BR, RWKV_IN)
    w_lo = lo[:, 0:64]
    a_lo = lo[:, 64:128]
    g_lo = lo[:, 128:256]

    wpre = w0_ref[...] + _dot(jnp.tanh(w_lo).astype(BF16), w2_ref[...])
    nw = -wpre
    softplus = jnp.maximum(nw, 0.0) + jnp.log1p(jnp.exp(-jnp.abs(nw)))
    w = -softplus - 0.5
    lw = -jnp.exp(w)
    a = jax.nn.sigmoid(a0_ref[...] + _dot(a_lo.astype(BF16), a2_ref[...]))
    g = _dot(jax.nn.sigmoid(g_lo).astype(BF16), g2_ref[...])

    kkv = k * kk_ref[...]
    kkn = kkv / jnp.maximum(jnp.sqrt(_seg64_sum(kkv * kkv)), 1e-12)
    k2 = k * (1.0 + (a - 1.0) * ka_ref[...])
    b = kkn * a

    chunks = range(nrows // L)
    rows_of = lambda x, c: x[c * L:(c + 1) * L, :]
    cum = jnp.concatenate([_cumsum_rows(rows_of(lw, c)) for c in chunks], axis=0)
    cum_last = jnp.concatenate(
        [jnp.broadcast_to(cum[(c + 1) * L - 1:(c + 1) * L, :], (L, W_BR)) for c in chunks], axis=0)
    e_to_end = jnp.exp(cum_last - cum)
    e_inv = jnp.exp(-cum)
    kkd = kkn * jnp.exp(cum - lw)
    rd = r * jnp.exp(cum)
    bd = b * e_inv
    kd = k2 * e_inv
    bd2 = b * e_to_end
    kd2 = k2 * e_to_end

    li = lax.broadcasted_iota(jnp.int32, (P, P), 0)
    lj = lax.broadcasted_iota(jnp.int32, (P, P), 1)
    tt = li % L
    ts = lj % L
    strict = tt > ts
    incl = tt >= ts
    eye = jnp.where(li == lj, 1.0, 0.0)
    first_half = lax.broadcasted_iota(jnp.int32, (L, P), 1) < RWKV_DH

    def stack(x, c, p):
        xp = x[c * L:(c + 1) * L, p * P:(p + 1) * P]
        return jnp.concatenate([jnp.where(first_half, xp, 0.0), jnp.where(first_half, 0.0, xp)],
                               axis=0).astype(BF16)

    pairs = range(RWKV_HEADS // 2)
    cps = [(c, p) for c in chunks for p in pairs]
    kkm = {cp: stack(kkd, *cp) for cp in cps}
    rm = {cp: stack(rd, *cp) for cp in cps}
    vm = {cp: stack(v, *cp) for cp in cps}
    gram = {cp: _dot_nt(jnp.concatenate([kkm[cp], rm[cp]], axis=0),
                        jnp.concatenate([stack(bd, *cp), stack(kd, *cp)], axis=0)) for cp in cps}
    a_bb = {cp: jnp.where(strict, gram[cp][:P, :P], 0.0) for cp in cps}
    av = {cp: _dot(jnp.where(strict, gram[cp][:P, P:], 0.0).astype(BF16), vm[cp]) for cp in cps}
    r_bk = {cp: jnp.concatenate([jnp.where(incl, gram[cp][P:, :P], 0.0), jnp.where(incl, gram[cp][P:, P:], 0.0)],
                                axis=1).astype(BF16) for cp in cps}
    lvl = (tt // 2) == (ts // 2)
    t_inv = {cp: eye - jnp.where(lvl, a_bb[cp], 0.0) for cp in cps}
    blk = 2
    while blk < L:
        lvl = ((tt // (2 * blk)) == (ts // (2 * blk))) & ((tt // blk) != (ts // blk))
        t_bf = {cp: t_inv[cp].astype(BF16) for cp in cps}
        mt = {cp: _dot(jnp.where(lvl, a_bb[cp], 0.0).astype(BF16), t_bf[cp]).astype(BF16) for cp in cps}
        t_inv = {cp: t_inv[cp] - _dot(t_bf[cp], mt[cp]) for cp in cps}
        blk *= 2
    t_bf = {cp: t_inv[cp].astype(BF16) for cp in cps}

    state = [s_ref[p] for p in pairs]
    y_rows = []
    for c in chunks:
        s_bf = [state[p].astype(BF16) for p in pairs]
        wm = [(_dot_nt(kkm[c, p], s_bf[p]) + av[c, p]).astype(BF16) for p in pairs]
        ys0 = [_dot_nt(rm[c, p], s_bf[p]) for p in pairs]
        bkm2 = [jnp.concatenate([stack(bd2, c, p), stack(kd2, c, p)], axis=0) for p in pairs]
        p_last = jnp.exp(cum[(c + 1) * L - 1:(c + 1) * L, :])
        uv = [jnp.concatenate([(-_dot(t_bf[c, p], wm[p])).astype(BF16), vm[c, p]], axis=0)
              for p in pairs]
        state = [state[p] * p_last[:, p * P:(p + 1) * P] + _dot_tn(uv[p], bkm2[p]) for p in pairs]
        ys = [ys0[p] + _dot(r_bk[c, p], uv[p]) for p in pairs]
        y_rows.append(jnp.concatenate([y[:L] + y[L:] for y in ys], axis=1))
    for p in pairs:
        s_ref[p] = state[p]
    y = jnp.concatenate(y_rows, axis=0)

    inv_n = 1.0 / RWKV_DH
    mu = _seg64_sum(y) * inv_n
    d = y - mu
    var = _seg64_sum(d * d) * inv_n
    yn = d * lax.rsqrt(var + RWKV_GN_EPS) * ng_ref[...] + nb_ref[...]
    bonus = _seg64_sum(r * k2 * rk_ref[...]) * v
    o_ref[...] = ((yn + bonus) * g).astype(BF16)


def _rwkv(proj, prm, layer):
    t = proj.shape[0]
    tr = min(RWKV_ROWS, t)
    lo_w = RWKV_IN - 3 * W_BR
    vec = lambda n: pl.BlockSpec((None, 1, n), lambda i: (layer, 0, 0))
    mat = lambda r: pl.BlockSpec((None, r, W_BR), lambda i: (layer, 0, 0))
    wide = lambda j: pl.BlockSpec((tr, W_BR), lambda i, j=j: (i, OFF_RWKV // W_BR + j))
    return pl.pallas_call(
        _rwkv_kernel,
        grid=(t // tr,),
        in_specs=[wide(0), wide(1), wide(2),
                  pl.BlockSpec((tr, lo_w), lambda i: (i, (OFF_RWKV + 3 * W_BR) // lo_w)), vec(RWKV_IN)]
                 + [vec(W_BR)] * 7 + [mat(64), mat(64), mat(128)],
        out_specs=pl.BlockSpec((tr, W_BR), lambda i: (i, 0)),
        out_shape=jax.ShapeDtypeStruct((t, W_BR), BF16),
        scratch_shapes=[pltpu.VMEM((1, RWKV_IN), F32),
                        pltpu.VMEM((RWKV_HEADS // 2, 2 * RWKV_DH, 2 * RWKV_DH), F32)],
        compiler_params=_params(("arbitrary",), 32),
        name="rwkv7",
    )(proj, proj, proj, proj, prm["mu"], prm["w0"], prm["a0"], prm["k_k"], prm["k_a"], prm["r_k"], prm["ng"],
      prm["nb"], prm["w2"], prm["a2"], prm["g2"])


def _gelu_tanh(x):
    c = math.sqrt(2.0 / math.pi)
    return x * (0.5 * (1.0 + jnp.tanh(c * (x + 0.044715 * (x * x * x)))))


def _s5_kernel(*refs):
    C = SSM_CHUNK
    u_refs = refs[:C]
    (lre_ref, lim_ref, ls_ref, bcat_ref, bswp_ref, ccat_ref, cswp_ref, d_ref, z_ref,
     et_ref, bb_ref, fbig_ref, wot_ref, mbig_ref, x_ref, prev_ref, a_ref) = refs[C:]
    GL = SSM_GROUP * SSM_GB
    PS = SSM_STATE
    half = SSM_GB * PS
    n = u_refs[0].shape[0]

    @pl.when(pl.program_id(0) == 0)
    def _():
        fbig_ref[...] = jnp.zeros_like(fbig_ref)
        wot_ref[...] = jnp.zeros_like(wot_ref)
        mbig_ref[...] = jnp.zeros_like(mbig_ref)

    first1 = lax.broadcasted_iota(jnp.int32, (1, GL), 1) < PS
    sgn = jnp.where(first1, -1.0, 1.0)
    first16 = lax.broadcasted_iota(jnp.int32, (SSM_GROUP, GL), 1) < PS
    tau = lax.broadcasted_iota(jnp.int32, (24, GL), 0).astype(F32)

    for g in range(SSM_GB):
        lr = lre_ref[g:g + 1, :]
        li = lim_ref[g:g + 1, :]
        dt = jnp.exp(ls_ref[g:g + 1, :])
        mag = jnp.exp(lr * dt)
        ab_re = mag * jnp.cos(li * dt)
        ab_im = mag * jnp.sin(li * dt)
        denom = lr * lr + li * li
        f_re = ((ab_re - 1.0) * lr + ab_im * li) / denom
        f_im = (ab_im * lr - (ab_re - 1.0) * li) / denom
        bcat = bcat_ref[g]
        bswp = bswp_ref[g]
        bbs = f_re * bcat + (sgn * f_im) * bswp
        bbw = f_re * bswp - (sgn * f_im) * bcat
        pmag = jnp.exp(tau * (lr * dt))
        pw_re = pmag * jnp.cos(tau * (li * dt))
        pw_im = pmag * jnp.sin(tau * (li * dt))
        ccat = ccat_ref[g]
        cswp = cswp_ref[g]
        rows = slice(g * SSM_GROUP, (g + 1) * SSM_GROUP)
        tile = slice((g // 2) * GL, (g // 2 + 1) * GL)
        tile_im = slice(half + (g // 2) * GL, half + (g // 2 + 1) * GL)
        odd = g % 2

        def planes(xy, yx):
            if odd:
                return jnp.where(first16, 0.0, yx).astype(BF16), jnp.where(first16, 0.0, xy).astype(BF16)
            return jnp.where(first16, xy, 0.0).astype(BF16), jnp.where(first16, yx, 0.0).astype(BF16)

        for t in range(C + 1):
            pr = pw_re[t:t + 1, :]
            pi = pw_im[t:t + 1, :]
            e = pr * (-sgn * ccat) - pi * cswp
            et_ref[t, rows, :] = e
            if t >= 1:
                esw = pr * (sgn * cswp) - pi * ccat
                o_re, o_im = planes(e, esw)
                orow = slice((t - 1) * GL + g * SSM_GROUP, (t - 1) * GL + (g + 1) * SSM_GROUP)
                wot_ref[orow, tile] = o_re
                wot_ref[orow, tile_im] = o_im
            if t < C:
                qr = pw_re[C - 1 - t:C - t, :]
                qi = pw_im[C - 1 - t:C - t, :]
                f = qr * bbs + qi * (sgn * bbw)
                fsw = qr * bbw - qi * (sgn * bbs)
                i_re, i_im = planes(f, fsw)
                irow = slice(t * GL + g * SSM_GROUP, t * GL + (g + 1) * SSM_GROUP)
                fbig_ref[irow, tile] = i_re
                fbig_ref[irow, tile_im] = i_im
        bb_ref[rows, :] = bbs
        lanes = slice(odd * PS, (odd + 1) * PS)
        dst = slice((g // 2) * GL + odd * PS, (g // 2) * GL + (odd + 1) * PS)
        a_ref[0:1, dst] = pw_re[C:C + 1, lanes]
        a_ref[1:2, dst] = pw_im[C:C + 1, lanes]

    bi = lax.broadcasted_iota(jnp.int32, (GL, GL), 0) // SSM_GROUP
    bj = lax.broadcasted_iota(jnp.int32, (GL, GL), 1) // SSM_GROUP
    bh, bl = _split2(bb_ref[...])
    for t in range(C):
        eh, el = _split2(et_ref[t])
        kt = jnp.where(bi == bj, _dot_nt(bh, eh) + _dot_nt(bh, el) + _dot_nt(bl, eh), 0.0).astype(BF16)
        for j in range(C - t):
            mbig_ref[j * GL:(j + 1) * GL, (j + t) * GL:(j + t + 1) * GL] = kt

    ucat = jnp.concatenate([u_refs[j][...].astype(BF16) for j in range(C)], axis=1)
    x_ref[...] = _dot(ucat, fbig_ref[...])

    a_re = a_ref[0:1, :]
    a_im = a_ref[1:2, :]

    def step(i, carry):
        s_re, s_im = carry
        prev_ref[pl.ds(i, 1), 0:half] = s_re
        prev_ref[pl.ds(i, 1), half:2 * half] = s_im
        x_re = x_ref[pl.ds(i, 1), 0:half]
        x_im = x_ref[pl.ds(i, 1), half:2 * half]
        return (a_re * s_re - a_im * s_im + x_re, a_re * s_im + a_im * s_re + x_im)

    zero = jnp.zeros((1, half), F32)
    lax.fori_loop(0, n, step, (zero, zero), unroll=8)

    prev_bf = prev_ref[...].astype(BF16)
    steps_per_dot = 4
    for q in range(C // steps_per_dot):
        kq = (q + 1) * steps_per_dot * GL
        cols = slice(q * steps_per_dot * GL, (q + 1) * steps_per_dot * GL)
        y = _dot(ucat[:, :kq], mbig_ref[0:kq, cols]) + _dot_nt(prev_bf, wot_ref[cols, :])
        for ii in range(steps_per_dot):
            i = q * steps_per_dot + ii
            yi = y[:, ii * GL:(ii + 1) * GL] + d_ref[...] * u_refs[i][...]
            z_ref[i] = _gelu_tanh(yi).astype(BF16)


def _s5(proj, prm, layer):
    t, width = proj.shape
    C = SSM_CHUNK
    n = t // C
    gl = SSM_GROUP * SSM_GB
    nblk = W_BR // gl
    st = 2 * SSM_GB * SSM_STATE
    u2 = proj.reshape(n, C * width)
    u_specs = [pl.BlockSpec((n, gl), lambda b, j=j: (0, j * (width // gl) + b)) for j in range(C)]
    lam = pl.BlockSpec((None, SSM_GB, gl), lambda b: (layer, b, 0))
    bc = pl.BlockSpec((None, SSM_GB, SSM_GROUP, gl), lambda b: (layer, b, 0, 0))
    return pl.pallas_call(
        _s5_kernel,
        grid=(nblk,),
        in_specs=u_specs + [lam, lam, lam, bc, bc, bc, bc, pl.BlockSpec((None, 1, gl), lambda b: (layer, 0, b))],
        out_specs=pl.BlockSpec((C, n, gl), lambda b: (0, 0, b)),
        out_shape=jax.ShapeDtypeStruct((C, n, W_BR), BF16),
        scratch_shapes=[pltpu.VMEM((C + 1, gl, gl), F32), pltpu.VMEM((gl, gl), F32),
                        pltpu.VMEM((C * gl, st), BF16), pltpu.VMEM((C * gl, st), BF16),
                        pltpu.VMEM((C * gl, C * gl), BF16),
                        pltpu.VMEM((n, st), F32), pltpu.VMEM((n, st), F32), pltpu.VMEM((8, st // 2), F32)],
        compiler_params=_params(("arbitrary",), 56),
        name="s5",
    )(*([u2] * C), prm["lam_re"], prm["lam_im"], prm["log_step"], prm["b_cat"], prm["b_swp"], prm["c_cat"],
      prm["c_swp"], prm["d"])


def _merge_kernel(z_ref, r_ref, k_ref, wa_ref, wb_ref, wr_ref, wk_ref, g0_ref, g1_ref, g2_ref, o_ref,
                  wa_bf, wb_bf, wr_bf, wk_bf):
    for w_ref, w_bf in ((wa_ref, wa_bf), (wb_ref, wb_bf), (wr_ref, wr_bf), (wk_ref, wk_bf)):
        _cast_once(w_ref, w_bf)
    sub = min(SUB_ROWS, z_ref.shape[0])
    for s in range(z_ref.shape[0] // sub):
        rows = slice(s * sub, (s + 1) * sub)
        z = z_ref[rows, :]
        y_ssm = _dot(z, wa_bf[...]) * jax.nn.sigmoid(_dot(z, wb_bf[...]))
        y_ret = _dot(r_ref[rows, :], wr_bf[...])
        y_rwkv = _dot(k_ref[rows, :], wk_bf[...])
        o_ref[rows, :] = (g0_ref[rows, :] * y_ssm + g1_ref[rows, :] * y_ret
                          + g2_ref[rows, :] * y_rwkv).astype(BF16)


def _merge(z, o_ret, o_rwkv, w_glu, w_ret, w_rwkv, gates, layer):
    t = z.shape[0]
    tm = min(512, t)
    tn = 512
    nb = D_MODEL // tn
    act = pl.BlockSpec((tm, W_BR), lambda j, i: (i, 0))
    wsp = lambda off: pl.BlockSpec((None, W_BR, tn), lambda j, i, off=off: (layer, 0, j + off))
    gsp = lambda off: pl.BlockSpec((tm, tn), lambda j, i, off=off: (i, j + off))
    return pl.pallas_call(
        _merge_kernel,
        grid=(nb, t // tm),
        in_specs=[act, act, act, wsp(0), wsp(nb), wsp(0), wsp(0), gsp(0), gsp(nb), gsp(2 * nb)],
        out_specs=pl.BlockSpec((tm, tn), lambda j, i: (i, j)),
        out_shape=jax.ShapeDtypeStruct((t, D_MODEL), BF16),
        scratch_shapes=[pltpu.VMEM((W_BR, tn), BF16)] * 4,
        compiler_params=_params(("parallel", "arbitrary"), 48),
        name="merge",
    )(z, o_ret, o_rwkv, w_glu, w_glu, w_ret, w_rwkv, gates, gates, gates)


def _residual_ln_rows(a_ref, w_bf, x_ref, g_ref, b_ref, o_ref, obf_ref):
    sub = min(128, a_ref.shape[0])
    for s in range(a_ref.shape[0] // sub):
        rows = slice(s * sub, (s + 1) * sub)
        y = DEEPNORM_ALPHA * x_ref[rows, :] + _dot(a_ref[rows, :], w_bf[...])
        out = _layer_norm(y, g_ref[...], b_ref[...])
        o_ref[rows, :] = out
        obf_ref[rows, :] = out.astype(BF16)


def _out_ln_kernel(m_ref, w_ref, x_ref, g_ref, b_ref, o_ref, obf_ref, w_bf):
    @pl.when(pl.program_id(0) == 0)
    def _():
        w_bf[...] = w_ref[...].astype(BF16)

    _residual_ln_rows(m_ref, w_bf, x_ref, g_ref, b_ref, o_ref, obf_ref)


def _out_ln(m, w_all, x, g, b, layer):
    t = x.shape[0]
    tm = min(256, t)
    row = pl.BlockSpec((tm, D_MODEL), lambda i: (i, 0))
    vec = pl.BlockSpec((None, 1, D_MODEL), lambda i: (layer, 0, 0))
    whole = pl.BlockSpec((None, D_MODEL, D_MODEL), lambda i: (layer, 0, 0), pipeline_mode=pl.Buffered(1))
    return pl.pallas_call(
        _out_ln_kernel,
        grid=(t // tm,),
        in_specs=[row, whole, row, vec, vec],
        out_specs=[row, row],
        out_shape=[jax.ShapeDtypeStruct((t, D_MODEL), F32), jax.ShapeDtypeStruct((t, D_MODEL), BF16)],
        scratch_shapes=[pltpu.VMEM((D_MODEL, D_MODEL), BF16)],
        compiler_params=_params(("arbitrary",), 48),
        name="out_ln",
    )(m, w_all, x, g, b)


def _ffn_up_kernel(x_ref, w1_ref, w2_ref, c1_ref, c2_ref, o_ref, h1_ref, h2_ref, w1_bf, w2_bf):
    tm = x_ref.shape[0]
    _cast_once(w1_ref, w1_bf)
    _cast_once(w2_ref, w2_bf)

    @pl.when(pl.program_id(1) == 0)
    def _():
        h1_ref[0:8, :] = jnp.zeros((8, h1_ref.shape[1]), F32)
        h2_ref[0:8, :] = jnp.zeros((8, h2_ref.shape[1]), F32)

    sub = min(SUB_ROWS, tm)
    for s in range(tm // sub):
        r0 = s * sub
        x = x_ref[r0:r0 + sub, :]

        def conv(w_bf, c_ref, h_ref):
            h = _dot(x, w_bf[...])
            h_ref[8 + r0:8 + r0 + sub, :] = h
            return (c_ref[2:3, :] * h + c_ref[1:2, :] * h_ref[7 + r0:7 + r0 + sub, :]
                    + c_ref[0:1, :] * h_ref[6 + r0:6 + r0 + sub, :])

        a = conv(w1_bf, c1_ref, h1_ref)
        b = conv(w2_bf, c2_ref, h2_ref)
        o_ref[r0:r0 + sub, :] = (a * jax.nn.sigmoid(a) * b).astype(BF16)
    h1_ref[0:8, :] = h1_ref[tm:tm + 8, :]
    h2_ref[0:8, :] = h2_ref[tm:tm + 8, :]


def _ffn_up(x_bf, w_up, w_conv, layer):
    t = x_bf.shape[0]
    tm = min(1024, t)
    tn = 512
    nb = D_FF // tn
    wsp = lambda off: pl.BlockSpec((None, D_MODEL, tn), lambda j, i, off=off: (layer, 0, j + off))
    csp = lambda off: pl.BlockSpec((None, 3, tn), lambda j, i, off=off: (layer, 0, j + off))
    return pl.pallas_call(
        _ffn_up_kernel,
        grid=(nb, t // tm),
        in_specs=[pl.BlockSpec((tm, D_MODEL), lambda j, i: (i, 0)), wsp(0), wsp(nb), csp(0), csp(nb)],
        out_specs=pl.BlockSpec((tm, tn), lambda j, i: (i, j)),
        out_shape=jax.ShapeDtypeStruct((t, D_FF), BF16),
        scratch_shapes=[pltpu.VMEM((tm + 8, tn), F32), pltpu.VMEM((tm + 8, tn), F32),
                        pltpu.VMEM((D_MODEL, tn), BF16), pltpu.VMEM((D_MODEL, tn), BF16)],
        compiler_params=_params(("parallel", "arbitrary"), 48),
        name="ffn_up",
    )(x_bf, w_up, w_up, w_conv, w_conv)


def _ffn_down_kernel(a_ref, w_ref, x_ref, g_ref, b_ref, o_ref, obf_ref):
    _residual_ln_rows(a_ref, w_ref, x_ref, g_ref, b_ref, o_ref, obf_ref)


def _ffn_down(act, w_all, x, g, b, layer):
    t = x.shape[0]
    tm = min(256, t)
    row = pl.BlockSpec((tm, D_MODEL), lambda i: (i, 0))
    vec = pl.BlockSpec((None, 1, D_MODEL), lambda i: (layer, 0, 0))
    whole = pl.BlockSpec((None, D_FF, D_MODEL), lambda i: (layer, 0, 0), pipeline_mode=pl.Buffered(1))
    return pl.pallas_call(
        _ffn_down_kernel,
        grid=(t // tm,),
        in_specs=[pl.BlockSpec((tm, D_FF), lambda i: (i, 0)), whole, row, vec, vec],
        out_specs=[row, row],
        out_shape=[jax.ShapeDtypeStruct((t, D_MODEL), F32), jax.ShapeDtypeStruct((t, D_MODEL), BF16)],
        compiler_params=_params(("parallel",), 48),
        name="ffn_down",
    )(act, w_all, x, g, b)


def kernel(x, positions, w_in, ssm_lambda_re, ssm_lambda_im, ssm_log_step, ssm_b_re, ssm_b_im, ssm_c_re, ssm_c_im, ssm_d, ssm_glu, ret_norm_g, ret_norm_b, ret_out, rwkv_mu, rwkv_w0, rwkv_w2, rwkv_a0, rwkv_a2, rwkv_g2, rwkv_k_k, rwkv_k_a, rwkv_r_k, rwkv_norm_g, rwkv_norm_b, rwkv_out, w_o, ln1_g, ln1_b, ffn_up, ffn_conv, ffn_down, ln2_g, ln2_b):
    bsz, t, _ = x.shape
    assert bsz == 1
    depth = w_in.shape[0]
    vec3 = lambda p: p.reshape(depth, 1, p.shape[-1])
    down_bf = ffn_down.astype(BF16)

    twice = lambda a, b: jnp.concatenate([a, b], axis=-1)
    b_re_t, b_im_t = jnp.swapaxes(ssm_b_re, -1, -2), jnp.swapaxes(ssm_b_im, -1, -2)
    log_step = jnp.broadcast_to(ssm_log_step[..., None], ssm_lambda_re.shape)
    s5_prm = dict(
        lam_re=twice(ssm_lambda_re, ssm_lambda_re), lam_im=twice(ssm_lambda_im, ssm_lambda_im),
        log_step=twice(log_step, log_step),
        b_cat=twice(b_re_t, b_im_t), b_swp=twice(b_im_t, b_re_t),
        c_cat=twice(ssm_c_re, ssm_c_im), c_swp=twice(ssm_c_im, ssm_c_re),
        d=vec3(ssm_d))
    rwkv_prm = dict(mu=vec3(rwkv_mu), w0=vec3(rwkv_w0), a0=vec3(rwkv_a0), k_k=vec3(rwkv_k_k), k_a=vec3(rwkv_k_a),
                    r_k=vec3(rwkv_r_k), ng=vec3(rwkv_norm_g), nb=vec3(rwkv_norm_b),
                    w2=rwkv_w2.astype(BF16), a2=rwkv_a2.astype(BF16), g2=rwkv_g2.astype(BF16))
    ret_g, ret_b = vec3(ret_norm_g), vec3(ret_norm_b)
    ln1g, ln1b, ln2g, ln2b = vec3(ln1_g), vec3(ln1_b), vec3(ln2_g), vec3(ln2_b)

    cos_t, sin_t = _rope_tables(positions)
    xf = x.reshape(t, D_MODEL)
    xb = xf.astype(BF16)
    for l in range(depth):
        proj = _proj(xb, w_in, l, 0, OFF_GATE, 768)
        gates = _proj(xb, w_in, l, OFF_GATE, N_GATE, 768, act="sigmoid")
        z = _s5(proj, s5_prm, l).transpose(1, 0, 2).reshape(t, W_BR)
        o_ret = _retention(proj, cos_t, sin_t, ret_g, ret_b, l)
        o_rwkv = _rwkv(proj, rwkv_prm, l)
        merged = _merge(z, o_ret, o_rwkv, ssm_glu, ret_out, rwkv_out, gates, l)
        xf, xb = _out_ln(merged, w_o, xf, ln1g, ln1b, l)
        act = _ffn_up(xb, ffn_up, ffn_conv, l)
        xf, xb = _ffn_down(act, down_bf, xf, ln2g, ln2b, l)
    return xf.reshape(bsz, t, D_MODEL)
```

```python
import functools
import math

import jax
import jax.numpy as jnp
from jax import lax
from jax.experimental import pallas as pl
from jax.experimental.pallas import tpu as pltpu

F32 = jnp.float32
BF16 = jnp.bfloat16

D_MODEL = 2048
DEPTH = 4
W_BR = 1024
SSM_GROUP = 16
SSM_GROUPS = 64
SSM_STATE = 64
SSM_CHUNK = 16
SSM_GB = 8
RET_HEADS = 8
RET_DH = 128
RET_CHUNK = 128
ROPE_BASE = 10000.0
RWKV_HEADS = 16
RWKV_DH = 64
RWKV_CHUNK = 64
RWKV_ROWS = 256
RWKV_IN = 3328
D_FF = 5632
OFF_RET = 1024
OFF_RWKV = 5120
OFF_GATE = 8448
N_GATE = 3 * D_MODEL
DEEPNORM_ALPHA = (2.0 * DEPTH) ** 0.25
LN_EPS = 1e-5
GN_EPS = 1e-5
RWKV_GN_EPS = 64e-5
MIB = 1024 * 1024
SUB_ROWS = 256


def _params(sem, vmem_mib):
    return pltpu.CompilerParams(dimension_semantics=sem, vmem_limit_bytes=vmem_mib * MIB)


def _dot(a, b):
    return jnp.dot(a, b, preferred_element_type=F32)


def _dot_nt(a, b):
    return lax.dot_general(a, b, (((1,), (1,)), ((), ())), preferred_element_type=F32)


def _dot_tn(a, b):
    return lax.dot_general(a, b, (((0,), (0,)), ((), ())), preferred_element_type=F32)


def _split2(x):
    hi = x.astype(BF16)
    lo = (x - hi.astype(F32)).astype(BF16)
    return hi, lo


def _split3(x):
    hi = x.astype(BF16)
    r1 = x - hi.astype(F32)
    mid = r1.astype(BF16)
    lo = (r1 - mid.astype(F32)).astype(BF16)
    return hi, mid, lo


def _layer_norm(y, g, b):
    mu = jnp.mean(y, axis=-1, keepdims=True)
    d = y - mu
    var = jnp.mean(d * d, axis=-1, keepdims=True)
    return d * lax.rsqrt(var + LN_EPS) * g + b


def _cast_once(w_ref, wbf_ref):
    @pl.when(pl.program_id(1) == 0)
    def _():
        wbf_ref[...] = w_ref[...].astype(BF16)


def _proj_kernel(x_ref, w_ref, *rest):
    if len(rest) == 4:
        g_ref, o_ref, go_ref, wbf_ref = rest
    else:
        o_ref, wbf_ref = rest
    _cast_once(w_ref, wbf_ref)
    o_ref[...] = _dot(x_ref[...], wbf_ref[...])
    if len(rest) == 4:
        go_ref[...] = jax.nn.sigmoid(g_ref[...])


def _proj(x_bf, w_all, layer, col0, ncols, tn, raw_gates=None):
    t, k = x_bf.shape
    tm = min(1024, t)
    off = col0 // tn
    assert off * tn == col0 and ncols % tn == 0
    in_specs = [pl.BlockSpec((tm, k), lambda j, i: (i, 0)),
                pl.BlockSpec((None, k, tn), lambda j, i: (layer, 0, j + off))]
    out_specs = pl.BlockSpec((tm, tn), lambda j, i: (i, j))
    out_shape = jax.ShapeDtypeStruct((t, ncols), F32)
    args = (x_bf, w_all)
    if raw_gates is not None:
        last = raw_gates.shape[1] // tn - 1
        assert last < ncols // tn
        side = pl.BlockSpec((tm, tn), lambda j, i: (i, jnp.minimum(j, last)))
        in_specs, out_specs = in_specs + [side], [out_specs, side]
        out_shape = [out_shape, jax.ShapeDtypeStruct(raw_gates.shape, F32)]
        args = args + (raw_gates,)
    return pl.pallas_call(
        _proj_kernel,
        grid=(ncols // tn, t // tm),
        in_specs=in_specs,
        out_specs=out_specs,
        out_shape=out_shape,
        scratch_shapes=[pltpu.VMEM((k, tn), BF16)],
        compiler_params=_params(("parallel", "arbitrary"), 56),
        name="proj",
    )(*args)


def _rope_kernel(pos_ref, cos_ref, sin_ref):
    half = RET_DH // 2
    pos = pos_ref[...].astype(F32)
    lane = lax.broadcasted_iota(jnp.int32, (1, RET_DH), 1)
    idx = jnp.where(lane < half, lane, lane - half).astype(F32)
    inv_freq = jnp.exp(-(idx / half) * math.log(ROPE_BASE))
    ang = pos * inv_freq
    cos_ref[...] = jnp.cos(ang)
    s = jnp.sin(ang)
    sin_ref[...] = jnp.where(lane < half, -s, s)


def _rope_tables(positions):
    t = positions.shape[-1]
    tm = min(1024, t)
    pos = positions.reshape(t, 1)
    return pl.pallas_call(
        _rope_kernel,
        grid=(t // tm,),
        in_specs=[pl.BlockSpec((tm, 1), lambda i: (i, 0))],
        out_specs=[pl.BlockSpec((tm, RET_DH), lambda i: (i, 0))] * 2,
        out_shape=[jax.ShapeDtypeStruct((t, RET_DH), F32)] * 2,
        compiler_params=_params(("parallel",), 16),
        name="rope",
    )(pos)


_RET_LOG_GAMMA = [math.log1p(-(2.0 ** (-5.0 - h))) for h in range(RET_HEADS)]


def _ret_kernel(q_ref, k_ref, v_ref, g_ref, cos_ref, sin_ref, ng_ref, nb_ref, o_ref, state_ref):
    c = RET_CHUNK

    @pl.when(pl.program_id(0) == 0)
    def _():
        state_ref[...] = jnp.zeros_like(state_ref)

    cos = cos_ref[...]
    sin = sin_ref[...]
    ri = lax.broadcasted_iota(jnp.int32, (c, c), 0).astype(F32)
    ci = lax.broadcasted_iota(jnp.int32, (c, c), 1).astype(F32)
    rel = ri - ci
    scale = RET_DH ** -0.5
    for h in range(RET_HEADS):
        sl = slice(h * RET_DH, (h + 1) * RET_DH)
        lg = _RET_LOG_GAMMA[h]
        q = q_ref[:, sl]
        k = k_ref[:, sl]
        qr = q * cos + pltpu.roll(q, RET_DH // 2, 1) * sin
        kr = (k * cos + pltpu.roll(k, RET_DH // 2, 1) * sin) * scale
        v_bf = v_ref[:, sl].astype(BF16)
        decay = jnp.where(rel >= 0, jnp.exp(lg * jnp.maximum(rel, 0.0)), 0.0)
        q_bf = qr.astype(BF16)
        scores = _dot_nt(q_bf, kr.astype(BF16)) * decay
        inner = _dot(scores.astype(BF16), v_bf)
        state = state_ref[h]
        cross = _dot(q_bf, state.astype(BF16)) * jnp.exp(lg * (ri + 1.0))
        k_dec = (kr * jnp.exp(lg * (c - 1.0 - ri))).astype(BF16)
        state_ref[h] = math.exp(lg * c) * state + _dot_tn(k_dec, v_bf)
        o = inner + cross
        mu = jnp.mean(o, axis=-1, keepdims=True)
        d = o - mu
        var = jnp.mean(d * d, axis=-1, keepdims=True)
        on = d * lax.rsqrt(var + GN_EPS) * ng_ref[:, sl] + nb_ref[:, sl]
        g = g_ref[:, sl]
        o_ref[:, sl] = (g * jax.nn.sigmoid(g) * on).astype(BF16)


def _retention(proj, cos_t, sin_t, ng, nb, layer):
    ret = proj
    t = ret.shape[0]
    c = RET_CHUNK
    col = lambda j: pl.BlockSpec((c, W_BR), lambda i, j=j: (i, j + OFF_RET // W_BR))
    vec = pl.BlockSpec((None, 1, W_BR), lambda i: (layer, 0, 0))
    tab = pl.BlockSpec((c, RET_DH), lambda i: (i, 0))
    return pl.pallas_call(
        _ret_kernel,
        grid=(t // c,),
        in_specs=[col(0), col(1), col(2), col(3), tab, tab, vec, vec],
        out_specs=pl.BlockSpec((c, W_BR), lambda i: (i, 0)),
        out_shape=jax.ShapeDtypeStruct((t, W_BR), BF16),
        scratch_shapes=[pltpu.VMEM((RET_HEADS, RET_DH, RET_DH), F32)],
        compiler_params=_params(("arbitrary",), 32),
        name="retention",
    )(ret, ret, ret, ret, cos_t, sin_t, ng, nb)


def _seg64_sum(x):
    tile = 2 * RWKV_DH
    first = lax.broadcasted_iota(jnp.int32, (x.shape[0], tile), 1) < RWKV_DH
    outs = []
    for t in range(x.shape[1] // tile):
        xt = x[:, t * tile:(t + 1) * tile]
        s0 = jnp.sum(jnp.where(first, xt, 0.0), axis=-1, keepdims=True)
        s1 = jnp.sum(jnp.where(first, 0.0, xt), axis=-1, keepdims=True)
        outs.append(jnp.where(first, s0, s1))
    return jnp.concatenate(outs, axis=1)


def _cumsum_rows(x):
    rows = lax.broadcasted_iota(jnp.int32, x.shape, 0)
    s = 1
    while s < x.shape[0]:
        x = x + jnp.where(rows >= s, pltpu.roll(x, s, 0), 0.0)
        s *= 2
    return x


def _rwkv_kernel(r_ref, k_ref, v_ref, lo_ref, mu_ref, w0_ref, a0_ref, kk_ref, ka_ref, rk_ref, ng_ref, nb_ref,
                 w2_ref, a2_ref, g2_ref, o_ref, last_ref, s_ref):
    L = RWKV_CHUNK
    P = 2 * RWKV_DH
    nrows = r_ref.shape[0]

    @pl.when(pl.program_id(0) == 0)
    def _():
        last_ref[...] = jnp.zeros_like(last_ref)
        s_ref[...] = jnp.zeros_like(s_ref)

    def token_shift(z_ref, c0, c1):
        z = z_ref[...]
        rows = lax.broadcasted_iota(jnp.int32, z.shape, 0)
        zprev = jnp.where(rows == 0, last_ref[:, c0:c1], pltpu.roll(z, 1, 0))
        last_ref[:, c0:c1] = z[nrows - 1:, :]
        return z + mu_ref[:, c0:c1] * (zprev - z)

    r = token_shift(r_ref, 0, W_BR)
    k = token_shift(k_ref, W_BR, 2 * W_BR)
    v = token_shift(v_ref, 2 * W_BR, 3 * W_BR)
    lo = token_shift(lo_ref, 3 * W_BR, RWKV_IN)
    w_lo = lo[:, 0:64]
    a_lo = lo[:, 64:128]
    g_lo = lo[:, 128:256]

    wpre = w0_ref[...] + _dot(jnp.tanh(w_lo).astype(BF16), w2_ref[...])
    nw = -wpre
    softplus = jnp.maximum(nw, 0.0) + jnp.log1p(jnp.exp(-jnp.abs(nw)))
    w = -softplus - 0.5
    lw = -jnp.exp(w)
    a = jax.nn.sigmoid(a0_ref[...] + _dot(a_lo.astype(BF16), a2_ref[...]))
    g = _dot(jax.nn.sigmoid(g_lo).astype(BF16), g2_ref[...])

    kkv = k * kk_ref[...]
    kkn = kkv / jnp.maximum(jnp.sqrt(_seg64_sum(kkv * kkv)), 1e-12)
    k2 = k * (1.0 + (a - 1.0) * ka_ref[...])
    b = kkn * a

    chunks = range(nrows // L)
    rows_of = lambda x, c: x[c * L:(c + 1) * L, :]
    cum = jnp.concatenate([_cumsum_rows(rows_of(lw, c)) for c in chunks], axis=0)
    cum_last = jnp.concatenate(
        [jnp.broadcast_to(cum[(c + 1) * L - 1:(c + 1) * L, :], (L, W_BR)) for c in chunks], axis=0)
    e_to_end = jnp.exp(cum_last - cum)
    e_inv = jnp.exp(-cum)
    kkd = kkn * jnp.exp(cum - lw)
    rd = r * jnp.exp(cum)
    bd = b * e_inv
    kd = k2 * e_inv
    bd2 = b * e_to_end
    kd2 = k2 * e_to_end

    li = lax.broadcasted_iota(jnp.int32, (P, P), 0)
    lj = lax.broadcasted_iota(jnp.int32, (P, P), 1)
    tt = li % L
    ts = lj % L
    strict = tt > ts
    incl = tt >= ts
    eye = jnp.where(li == lj, 1.0, 0.0)
    first_half = lax.broadcasted_iota(jnp.int32, (L, P), 1) < RWKV_DH

    def stack(x, c, p):
        xp = x[c * L:(c + 1) * L, p * P:(p + 1) * P]
        return jnp.concatenate([jnp.where(first_half, xp, 0.0), jnp.where(first_half, 0.0, xp)],
                               axis=0).astype(BF16)

    pairs = range(RWKV_HEADS // 2)
    cps = [(c, p) for c in chunks for p in pairs]
    kkm = {cp: stack(kkd, *cp) for cp in cps}
    rm = {cp: stack(rd, *cp) for cp in cps}
    vm = {cp: stack(v, *cp) for cp in cps}
    gram = {cp: _dot_nt(jnp.concatenate([kkm[cp], rm[cp]], axis=0),
                        jnp.concatenate([stack(bd, *cp), stack(kd, *cp)], axis=0)) for cp in cps}
    a_bb = {cp: jnp.where(strict, gram[cp][:P, :P], 0.0) for cp in cps}
    av = {cp: _dot(jnp.where(strict, gram[cp][:P, P:], 0.0).astype(BF16), vm[cp]) for cp in cps}
    r_bk = {cp: jnp.concatenate([jnp.where(incl, gram[cp][P:, :P], 0.0), jnp.where(incl, gram[cp][P:, P:], 0.0)],
                                axis=1).astype(BF16) for cp in cps}
    lvl = (tt // 2) == (ts // 2)
    t_inv = {cp: eye - jnp.where(lvl, a_bb[cp], 0.0) for cp in cps}
    blk = 2
    while blk < L:
        lvl = ((tt // (2 * blk)) == (ts // (2 * blk))) & ((tt // blk) != (ts // blk))
        t_bf = {cp: t_inv[cp].astype(BF16) for cp in cps}
        mt = {cp: _dot(jnp.where(lvl, a_bb[cp], 0.0).astype(BF16), t_bf[cp]).astype(BF16) for cp in cps}
        t_inv = {cp: t_inv[cp] - _dot(t_bf[cp], mt[cp]) for cp in cps}
        blk *= 2
    t_bf = {cp: t_inv[cp].astype(BF16) for cp in cps}

    state = [s_ref[p] for p in pairs]
    y_rows = []
    for c in chunks:
        s_bf = [state[p].astype(BF16) for p in pairs]
        wm = [(_dot_nt(kkm[c, p], s_bf[p]) + av[c, p]).astype(BF16) for p in pairs]
        ys0 = [_dot_nt(rm[c, p], s_bf[p]) for p in pairs]
        bkm2 = [jnp.concatenate([stack(bd2, c, p), stack(kd2, c, p)], axis=0) for p in pairs]
        p_last = jnp.exp(cum[(c + 1) * L - 1:(c + 1) * L, :])
        uv = [jnp.concatenate([(-_dot(t_bf[c, p], wm[p])).astype(BF16), vm[c, p]], axis=0)
              for p in pairs]
        state = [state[p] * p_last[:, p * P:(p + 1) * P] + _dot_tn(uv[p], bkm2[p]) for p in pairs]
        ys = [ys0[p] + _dot(r_bk[c, p], uv[p]) for p in pairs]
        y_rows.append(jnp.concatenate([y[:L] + y[L:] for y in ys], axis=1))
    for p in pairs:
        s_ref[p] = state[p]
    y = jnp.concatenate(y_rows, axis=0)

    inv_n = 1.0 / RWKV_DH
    mu = _seg64_sum(y) * inv_n
    d = y - mu
    var = _seg64_sum(d * d) * inv_n
    yn = d * lax.rsqrt(var + RWKV_GN_EPS) * ng_ref[...] + nb_ref[...]
    bonus = _seg64_sum(r * k2 * rk_ref[...]) * v
    o_ref[...] = ((yn + bonus) * g).astype(BF16)


def _rwkv(proj_a, proj_b, prm, layer):
    t = proj_a.shape[0]
    tr = min(RWKV_ROWS, t)
    lo_w = RWKV_IN - 3 * W_BR
    vec = lambda n: pl.BlockSpec((None, 1, n), lambda i: (layer, 0, 0))
    mat = lambda r: pl.BlockSpec((None, r, W_BR), lambda i: (layer, 0, 0))
    wide = lambda j: pl.BlockSpec((tr, W_BR), lambda i, j=j: (i, j))
    return pl.pallas_call(
        _rwkv_kernel,
        grid=(t // tr,),
        in_specs=[wide(OFF_RWKV // W_BR), wide(0), wide(1),
                  pl.BlockSpec((tr, lo_w), lambda i: (i, 2 * W_BR // lo_w)), vec(RWKV_IN)]
                 + [vec(W_BR)] * 7 + [mat(64), mat(64), mat(128)],
        out_specs=pl.BlockSpec((tr, W_BR), lambda i: (i, 0)),
        out_shape=jax.ShapeDtypeStruct((t, W_BR), BF16),
        scratch_shapes=[pltpu.VMEM((1, RWKV_IN), F32),
                        pltpu.VMEM((RWKV_HEADS // 2, 2 * RWKV_DH, 2 * RWKV_DH), F32)],
        compiler_params=_params(("arbitrary",), 32),
        name="rwkv7",
    )(proj_a, proj_b, proj_b, proj_b, prm["mu"], prm["w0"], prm["a0"], prm["k_k"], prm["k_a"], prm["r_k"],
      prm["ng"], prm["nb"], prm["w2"], prm["a2"], prm["g2"])


def _gelu_tanh(x):
    c = math.sqrt(2.0 / math.pi)
    return x * (0.5 * (1.0 + jnp.tanh(c * (x + 0.044715 * (x * x * x)))))


def _s5_kernel(u_ref, lre_ref, lim_ref, ls_ref, bcat_ref, bswp_ref, ccat_ref, cswp_ref, d_ref, z_ref,
               et_ref, bb_ref, fbig_ref, wot_ref, mbig_ref, x_ref, prev_ref, a_ref):
    C = SSM_CHUNK
    GL = SSM_GROUP * SSM_GB
    PS = SSM_STATE
    half = SSM_GB * PS
    n = u_ref.shape[0] // C
    u_step = lambda j: u_ref[pl.ds(j, n, stride=C), :]

    @pl.when(pl.program_id(0) == 0)
    def _():
        fbig_ref[...] = jnp.zeros_like(fbig_ref)
        wot_ref[...] = jnp.zeros_like(wot_ref)
        mbig_ref[...] = jnp.zeros_like(mbig_ref)

    first1 = lax.broadcasted_iota(jnp.int32, (1, GL), 1) < PS
    sgn = jnp.where(first1, -1.0, 1.0)
    first16 = lax.broadcasted_iota(jnp.int32, (SSM_GROUP, GL), 1) < PS
    tau = lax.broadcasted_iota(jnp.int32, (24, GL), 0).astype(F32)

    for g in range(SSM_GB):
        lr = lre_ref[g:g + 1, :]
        li = lim_ref[g:g + 1, :]
        dt = jnp.exp(ls_ref[g:g + 1, :])
        mag = jnp.exp(lr * dt)
        ab_re = mag * jnp.cos(li * dt)
        ab_im = mag * jnp.sin(li * dt)
        denom = lr * lr + li * li
        f_re = ((ab_re - 1.0) * lr + ab_im * li) / denom
        f_im = (ab_im * lr - (ab_re - 1.0) * li) / denom
        bcat = bcat_ref[g]
        bswp = bswp_ref[g]
        bbs = f_re * bcat + (sgn * f_im) * bswp
        bbw = f_re * bswp - (sgn * f_im) * bcat
        pmag = jnp.exp(tau * (lr * dt))
        pw_re = pmag * jnp.cos(tau * (li * dt))
        pw_im = pmag * jnp.sin(tau * (li * dt))
        ccat = ccat_ref[g]
        cswp = cswp_ref[g]
        rows = slice(g * SSM_GROUP, (g + 1) * SSM_GROUP)
        tile = slice((g // 2) * GL, (g // 2 + 1) * GL)
        tile_im = slice(half + (g // 2) * GL, half + (g // 2 + 1) * GL)
        odd = g % 2

        def planes(xy, yx):
            if odd:
                return jnp.where(first16, 0.0, yx).astype(BF16), jnp.where(first16, 0.0, xy).astype(BF16)
            return jnp.where(first16, xy, 0.0).astype(BF16), jnp.where(first16, yx, 0.0).astype(BF16)

        for t in range(C + 1):
            pr = pw_re[t:t + 1, :]
            pi = pw_im[t:t + 1, :]
            e = pr * (-sgn * ccat) - pi * cswp
            et_ref[t, rows, :] = e
            if t >= 1:
                esw = pr * (sgn * cswp) - pi * ccat
                o_re, o_im = planes(e, esw)
                orow = slice((t - 1) * GL + g * SSM_GROUP, (t - 1) * GL + (g + 1) * SSM_GROUP)
                wot_ref[orow, tile] = o_re
                wot_ref[orow, tile_im] = o_im
            if t < C:
                qr = pw_re[C - 1 - t:C - t, :]
                qi = pw_im[C - 1 - t:C - t, :]
                f = qr * bbs + qi * (sgn * bbw)
                fsw = qr * bbw - qi * (sgn * bbs)
                i_re, i_im = planes(f, fsw)
                irow = slice(t * GL + g * SSM_GROUP, t * GL + (g + 1) * SSM_GROUP)
                fbig_ref[irow, tile] = i_re
                fbig_ref[irow, tile_im] = i_im
        bb_ref[rows, :] = bbs
        lanes = slice(odd * PS, (odd + 1) * PS)
        dst = slice((g // 2) * GL + odd * PS, (g // 2) * GL + (odd + 1) * PS)
        a_ref[0:1, dst] = pw_re[C:C + 1, lanes]
        a_ref[1:2, dst] = pw_im[C:C + 1, lanes]

    bi = lax.broadcasted_iota(jnp.int32, (GL, GL), 0) // SSM_GROUP
    bj = lax.broadcasted_iota(jnp.int32, (GL, GL), 1) // SSM_GROUP
    bh, bl = _split2(bb_ref[...])
    for t in range(C):
        eh, el = _split2(et_ref[t])
        kt = jnp.where(bi == bj, _dot_nt(bh, eh) + _dot_nt(bh, el) + _dot_nt(bl, eh), 0.0).astype(BF16)
        for j in range(C - t):
            mbig_ref[j * GL:(j + 1) * GL, (j + t) * GL:(j + t + 1) * GL] = kt

    ucat = jnp.concatenate([u_step(j).astype(BF16) for j in range(C)], axis=1)
    x_ref[...] = _dot(ucat, fbig_ref[...])

    a_re = a_ref[0:1, :]
    a_im = a_ref[1:2, :]

    def step(i, carry):
        s_re, s_im = carry
        prev_ref[pl.ds(i, 1), 0:half] = s_re
        prev_ref[pl.ds(i, 1), half:2 * half] = s_im
        x_re = x_ref[pl.ds(i, 1), 0:half]
        x_im = x_ref[pl.ds(i, 1), half:2 * half]
        return (a_re * s_re - a_im * s_im + x_re, a_re * s_im + a_im * s_re + x_im)

    zero = jnp.zeros((1, half), F32)
    lax.fori_loop(0, n, step, (zero, zero), unroll=8)

    prev_bf = prev_ref[...].astype(BF16)
    steps_per_dot = 4
    for q in range(C // steps_per_dot):
        kq = (q + 1) * steps_per_dot * GL
        cols = slice(q * steps_per_dot * GL, (q + 1) * steps_per_dot * GL)
        y = _dot(ucat[:, :kq], mbig_ref[0:kq, cols]) + _dot_nt(prev_bf, wot_ref[cols, :])
        for ii in range(steps_per_dot):
            i = q * steps_per_dot + ii
            yi = y[:, ii * GL:(ii + 1) * GL] + d_ref[...] * u_step(i)
            z_ref[i] = _gelu_tanh(yi).astype(BF16)


def _s5(proj, prm, layer):
    t = proj.shape[0]
    C = SSM_CHUNK
    n = t // C
    gl = SSM_GROUP * SSM_GB
    nblk = W_BR // gl
    st = 2 * SSM_GB * SSM_STATE
    lam = pl.BlockSpec((None, SSM_GB, gl), lambda b: (layer, b, 0))
    bc = pl.BlockSpec((None, SSM_GB, SSM_GROUP, gl), lambda b: (layer, b, 0, 0))
    return pl.pallas_call(
        _s5_kernel,
        grid=(nblk,),
        in_specs=[pl.BlockSpec((t, gl), lambda b: (0, b)), lam, lam, lam, bc, bc, bc, bc,
                  pl.BlockSpec((None, 1, gl), lambda b: (layer, 0, b))],
        out_specs=pl.BlockSpec((C, n, gl), lambda b: (0, 0, b)),
        out_shape=jax.ShapeDtypeStruct((C, n, W_BR), BF16),
        scratch_shapes=[pltpu.VMEM((C + 1, gl, gl), F32), pltpu.VMEM((gl, gl), F32),
                        pltpu.VMEM((C * gl, st), BF16), pltpu.VMEM((C * gl, st), BF16),
                        pltpu.VMEM((C * gl, C * gl), BF16),
                        pltpu.VMEM((n, st), F32), pltpu.VMEM((n, st), F32), pltpu.VMEM((8, st // 2), F32)],
        compiler_params=_params(("arbitrary",), 56),
        name="s5",
    )(proj, prm["lam_re"], prm["lam_im"], prm["log_step"], prm["b_cat"], prm["b_swp"], prm["c_cat"],
      prm["c_swp"], prm["d"])


def _merge_kernel(z_ref, r_ref, k_ref, wa_ref, wb_ref, wr_ref, wk_ref, g0_ref, g1_ref, g2_ref, o_ref,
                  wa_bf, wb_bf, wr_bf, wk_bf):
    for w_ref, w_bf in ((wa_ref, wa_bf), (wb_ref, wb_bf), (wr_ref, wr_bf), (wk_ref, wk_bf)):
        _cast_once(w_ref, w_bf)
    sub = min(SUB_ROWS, z_ref.shape[0])
    for s in range(z_ref.shape[0] // sub):
        rows = slice(s * sub, (s + 1) * sub)
        z = z_ref[rows, :]
        y_ssm = _dot(z, wa_bf[...]) * jax.nn.sigmoid(_dot(z, wb_bf[...]))
        y_ret = _dot(r_ref[rows, :], wr_bf[...])
        y_rwkv = _dot(k_ref[rows, :], wk_bf[...])
        o_ref[rows, :] = (g0_ref[rows, :] * y_ssm + g1_ref[rows, :] * y_ret
                          + g2_ref[rows, :] * y_rwkv).astype(BF16)


def _merge(z, o_ret, o_rwkv, w_glu, w_ret, w_rwkv, gates, layer):
    t = z.shape[0]
    tm = min(512, t)
    tn = 512
    nb = D_MODEL // tn
    act = pl.BlockSpec((tm, W_BR), lambda j, i: (i, 0))
    wsp = lambda off: pl.BlockSpec((None, W_BR, tn), lambda j, i, off=off: (layer, 0, j + off))
    gsp = lambda off: pl.BlockSpec((tm, tn), lambda j, i, off=off: (i, j + off))
    return pl.pallas_call(
        _merge_kernel,
        grid=(nb, t // tm),
        in_specs=[act, act, act, wsp(0), wsp(nb), wsp(0), wsp(0), gsp(0), gsp(nb), gsp(2 * nb)],
        out_specs=pl.BlockSpec((tm, tn), lambda j, i: (i, j)),
        out_shape=jax.ShapeDtypeStruct((t, D_MODEL), BF16),
        scratch_shapes=[pltpu.VMEM((W_BR, tn), BF16)] * 4,
        compiler_params=_params(("parallel", "arbitrary"), 48),
        name="merge",
    )(z, o_ret, o_rwkv, w_glu, w_glu, w_ret, w_rwkv, gates, gates, gates)


def _residual_ln_rows(a_ref, w_bf, x_ref, g_ref, b_ref, o_ref, obf_ref):
    sub = min(128, a_ref.shape[0])
    for s in range(a_ref.shape[0] // sub):
        rows = slice(s * sub, (s + 1) * sub)
        y = DEEPNORM_ALPHA * x_ref[rows, :] + _dot(a_ref[rows, :], w_bf[...])
        out = _layer_norm(y, g_ref[...], b_ref[...])
        o_ref[rows, :] = out
        obf_ref[rows, :] = out.astype(BF16)


def _out_ln_kernel(m_ref, w_ref, x_ref, g_ref, b_ref, o_ref, obf_ref, w_bf):
    @pl.when(pl.program_id(0) == 0)
    def _():
        w_bf[...] = w_ref[...].astype(BF16)

    _residual_ln_rows(m_ref, w_bf, x_ref, g_ref, b_ref, o_ref, obf_ref)


def _out_ln(m, w_all, x, g, b, layer):
    t = x.shape[0]
    tm = min(256, t)
    row = pl.BlockSpec((tm, D_MODEL), lambda i: (i, 0))
    vec = pl.BlockSpec((None, 1, D_MODEL), lambda i: (layer, 0, 0))
    whole = pl.BlockSpec((None, D_MODEL, D_MODEL), lambda i: (layer, 0, 0), pipeline_mode=pl.Buffered(1))
    return pl.pallas_call(
        _out_ln_kernel,
        grid=(t // tm,),
        in_specs=[row, whole, row, vec, vec],
        out_specs=[row, row],
        out_shape=[jax.ShapeDtypeStruct((t, D_MODEL), F32), jax.ShapeDtypeStruct((t, D_MODEL), BF16)],
        scratch_shapes=[pltpu.VMEM((D_MODEL, D_MODEL), BF16)],
        compiler_params=_params(("arbitrary",), 48),
        name="out_ln",
    )(m, w_all, x, g, b)


def _ffn_up_kernel(x_ref, w1_ref, w2_ref, c1_ref, c2_ref, o_ref, h1_ref, h2_ref, w1_bf, w2_bf):
    tm = x_ref.shape[0]
    _cast_once(w1_ref, w1_bf)
    _cast_once(w2_ref, w2_bf)

    @pl.when(pl.program_id(1) == 0)
    def _():
        h1_ref[0:8, :] = jnp.zeros((8, h1_ref.shape[1]), F32)
        h2_ref[0:8, :] = jnp.zeros((8, h2_ref.shape[1]), F32)

    sub = tm
    for s in range(tm // sub):
        r0 = s * sub
        x = x_ref[r0:r0 + sub, :]

        def conv(w_bf, c_ref, h_ref):
            h = _dot(x, w_bf[...])
            h_ref[8 + r0:8 + r0 + sub, :] = h
            return (c_ref[2:3, :] * h + c_ref[1:2, :] * h_ref[7 + r0:7 + r0 + sub, :]
                    + c_ref[0:1, :] * h_ref[6 + r0:6 + r0 + sub, :])

        a = conv(w1_bf, c1_ref, h1_ref)
        b = conv(w2_bf, c2_ref, h2_ref)
        o_ref[r0:r0 + sub, :] = (a * jax.nn.sigmoid(a) * b).astype(BF16)
    h1_ref[0:8, :] = h1_ref[tm:tm + 8, :]
    h2_ref[0:8, :] = h2_ref[tm:tm + 8, :]


def _ffn_up(x_bf, w_up, w_conv, layer):
    t = x_bf.shape[0]
    tm = min(1024, t)
    tn = 512
    nb = D_FF // tn
    wsp = lambda off: pl.BlockSpec((None, D_MODEL, tn), lambda j, i, off=off: (layer, 0, j + off))
    csp = lambda off: pl.BlockSpec((None, 3, tn), lambda j, i, off=off: (layer, 0, j + off))
    return pl.pallas_call(
        _ffn_up_kernel,
        grid=(nb, t // tm),
        in_specs=[pl.BlockSpec((tm, D_MODEL), lambda j, i: (i, 0)), wsp(0), wsp(nb), csp(0), csp(nb)],
        out_specs=pl.BlockSpec((tm, tn), lambda j, i: (i, j)),
        out_shape=jax.ShapeDtypeStruct((t, D_FF), BF16),
        scratch_shapes=[pltpu.VMEM((tm + 8, tn), F32), pltpu.VMEM((tm + 8, tn), F32),
                        pltpu.VMEM((D_MODEL, tn), BF16), pltpu.VMEM((D_MODEL, tn), BF16)],
        compiler_params=_params(("parallel", "arbitrary"), 48),
        name="ffn_up",
    )(x_bf, w_up, w_up, w_conv, w_conv)


def _ffn_down_kernel(a_ref, w_ref, x_ref, g_ref, b_ref, o_ref, obf_ref):
    _residual_ln_rows(a_ref, w_ref, x_ref, g_ref, b_ref, o_ref, obf_ref)


def _ffn_down(act, w_all, x, g, b, layer):
    t = x.shape[0]
    tm = min(256, t)
    row = pl.BlockSpec((tm, D_MODEL), lambda i: (i, 0))
    vec = pl.BlockSpec((None, 1, D_MODEL), lambda i: (layer, 0, 0))
    whole = pl.BlockSpec((None, D_FF, D_MODEL), lambda i: (layer, 0, 0), pipeline_mode=pl.Buffered(1))
    return pl.pallas_call(
        _ffn_down_kernel,
        grid=(t // tm,),
        in_specs=[pl.BlockSpec((tm, D_FF), lambda i: (i, 0)), whole, row, vec, vec],
        out_specs=[row, row],
        out_shape=[jax.ShapeDtypeStruct((t, D_MODEL), F32), jax.ShapeDtypeStruct((t, D_MODEL), BF16)],
        compiler_params=_params(("parallel",), 48),
        name="ffn_down",
    )(act, w_all, x, g, b)


def kernel(x, positions, w_in, ssm_lambda_re, ssm_lambda_im, ssm_log_step, ssm_b_re, ssm_b_im, ssm_c_re, ssm_c_im, ssm_d, ssm_glu, ret_norm_g, ret_norm_b, ret_out, rwkv_mu, rwkv_w0, rwkv_w2, rwkv_a0, rwkv_a2, rwkv_g2, rwkv_k_k, rwkv_k_a, rwkv_r_k, rwkv_norm_g, rwkv_norm_b, rwkv_out, w_o, ln1_g, ln1_b, ffn_up, ffn_conv, ffn_down, ln2_g, ln2_b):
    bsz, t, _ = x.shape
    assert bsz == 1
    depth = w_in.shape[0]
    vec3 = lambda p: p.reshape(depth, 1, p.shape[-1])
    down_bf = ffn_down.astype(BF16)

    twice = lambda a, b: jnp.concatenate([a, b], axis=-1)
    b_re_t, b_im_t = jnp.swapaxes(ssm_b_re, -1, -2), jnp.swapaxes(ssm_b_im, -1, -2)
    log_step = jnp.broadcast_to(ssm_log_step[..., None], ssm_lambda_re.shape)
    s5_prm = dict(
        lam_re=twice(ssm_lambda_re, ssm_lambda_re), lam_im=twice(ssm_lambda_im, ssm_lambda_im),
        log_step=twice(log_step, log_step),
        b_cat=twice(b_re_t, b_im_t), b_swp=twice(b_im_t, b_re_t),
        c_cat=twice(ssm_c_re, ssm_c_im), c_swp=twice(ssm_c_im, ssm_c_re),
        d=vec3(ssm_d))
    rwkv_prm = dict(mu=vec3(rwkv_mu), w0=vec3(rwkv_w0), a0=vec3(rwkv_a0), k_k=vec3(rwkv_k_k), k_a=vec3(rwkv_k_a),
                    r_k=vec3(rwkv_r_k), ng=vec3(rwkv_norm_g), nb=vec3(rwkv_norm_b),
                    w2=rwkv_w2.astype(BF16), a2=rwkv_a2.astype(BF16), g2=rwkv_g2.astype(BF16))
    ret_g, ret_b = vec3(ret_norm_g), vec3(ret_norm_b)
    ln1g, ln1b, ln2g, ln2b = vec3(ln1_g), vec3(ln1_b), vec3(ln2_g), vec3(ln2_b)

    cos_t, sin_t = _rope_tables(positions)
    xf = x.reshape(t, D_MODEL)
    xb = xf.astype(BF16)
    for l in range(depth):
        raw_gates = _proj(xb, w_in, l, OFF_GATE, N_GATE, 768)
        split = OFF_RWKV + W_BR
        proj_a, gates = _proj(xb, w_in, l, 0, split, 768, raw_gates=raw_gates)
        proj_b = _proj(xb, w_in, l, split, OFF_GATE - split, 768)
        z = _s5(proj_a, s5_prm, l).transpose(1, 0, 2).reshape(t, W_BR)
        o_ret = _retention(proj_a, cos_t, sin_t, ret_g, ret_b, l)
        o_rwkv = _rwkv(proj_a, proj_b, rwkv_prm, l)
        merged = _merge(z, o_ret, o_rwkv, ssm_glu, ret_out, rwkv_out, gates, l)
        xf, xb = _out_ln(merged, w_o, xf, ln1g, ln1b, l)
        act = _ffn_up(xb, ffn_up, ffn_conv, l)
        xf, xb = _ffn_down(act, down_bf, xf, ln2g, ln2b, l)
    return xf.reshape(bsz, t, D_MODEL)
```

```python
import functools
import math

import jax
import jax.numpy as jnp
from jax import lax
from jax.experimental import pallas as pl
from jax.experimental.pallas import tpu as pltpu

F32 = jnp.float32
BF16 = jnp.bfloat16

D_MODEL = 2048
DEPTH = 4
W_BR = 1024
SSM_GROUP = 16
SSM_GROUPS = 64
SSM_STATE = 64
SSM_CHUNK = 16
SSM_GB = 8
RET_HEADS = 8
RET_DH = 128
RET_CHUNK = 128
ROPE_BASE = 10000.0
RWKV_HEADS = 16
RWKV_DH = 64
RWKV_CHUNK = 64
RWKV_ROWS = 256
RWKV_IN = 3328
D_FF = 5632
OFF_RET = 1024
OFF_RWKV = 5120
OFF_GATE = 8448
N_GATE = 3 * D_MODEL
DEEPNORM_ALPHA = (2.0 * DEPTH) ** 0.25
LN_EPS = 1e-5
GN_EPS = 1e-5
RWKV_GN_EPS = 64e-5
MIB = 1024 * 1024
SUB_ROWS = 256


def _params(sem, vmem_mib):
    return pltpu.CompilerParams(dimension_semantics=sem, vmem_limit_bytes=vmem_mib * MIB)


def _dot(a, b):
    return jnp.dot(a, b, preferred_element_type=F32)


def _dot_nt(a, b):
    return lax.dot_general(a, b, (((1,), (1,)), ((), ())), preferred_element_type=F32)


def _dot_tn(a, b):
    return lax.dot_general(a, b, (((0,), (0,)), ((), ())), preferred_element_type=F32)


def _split2(x):
    hi = x.astype(BF16)
    lo = (x - hi.astype(F32)).astype(BF16)
    return hi, lo


def _split3(x):
    hi = x.astype(BF16)
    r1 = x - hi.astype(F32)
    mid = r1.astype(BF16)
    lo = (r1 - mid.astype(F32)).astype(BF16)
    return hi, mid, lo


def _layer_norm(y, g, b):
    mu = jnp.mean(y, axis=-1, keepdims=True)
    d = y - mu
    var = jnp.mean(d * d, axis=-1, keepdims=True)
    return d * lax.rsqrt(var + LN_EPS) * g + b


def _cast_once(w_ref, wbf_ref):
    @pl.when(pl.program_id(1) == 0)
    def _():
        wbf_ref[...] = w_ref[...].astype(BF16)


def _proj_kernel(x_ref, w_ref, o_ref, wbf_ref):
    _cast_once(w_ref, wbf_ref)
    o_ref[...] = _dot(x_ref[...], wbf_ref[...])


def _proj(x_bf, w_all, layer, col0, ncols, tn):
    t, k = x_bf.shape
    tm = min(1024, t)
    off = col0 // tn
    assert off * tn == col0 and ncols % tn == 0
    return pl.pallas_call(
        _proj_kernel,
        grid=(ncols // tn, t // tm),
        in_specs=[pl.BlockSpec((tm, k), lambda j, i: (i, 0)),
                  pl.BlockSpec((None, k, tn), lambda j, i: (layer, 0, j + off))],
        out_specs=pl.BlockSpec((tm, tn), lambda j, i: (i, j)),
        out_shape=jax.ShapeDtypeStruct((t, ncols), F32),
        scratch_shapes=[pltpu.VMEM((k, tn), BF16)],
        compiler_params=_params(("parallel", "arbitrary"), 48),
        name="proj",
    )(x_bf, w_all)


def _rope_kernel(pos_ref, cos_ref, sin_ref):
    half = RET_DH // 2
    pos = pos_ref[...].astype(F32)
    lane = lax.broadcasted_iota(jnp.int32, (1, RET_DH), 1)
    idx = jnp.where(lane < half, lane, lane - half).astype(F32)
    inv_freq = jnp.exp(-(idx / half) * math.log(ROPE_BASE))
    ang = pos * inv_freq
    cos_ref[...] = jnp.cos(ang)
    s = jnp.sin(ang)
    sin_ref[...] = jnp.where(lane < half, -s, s)


def _rope_tables(positions):
    t = positions.shape[-1]
    tm = min(1024, t)
    pos = positions.reshape(t, 1)
    return pl.pallas_call(
        _rope_kernel,
        grid=(t // tm,),
        in_specs=[pl.BlockSpec((tm, 1), lambda i: (i, 0))],
        out_specs=[pl.BlockSpec((tm, RET_DH), lambda i: (i, 0))] * 2,
        out_shape=[jax.ShapeDtypeStruct((t, RET_DH), F32)] * 2,
        compiler_params=_params(("parallel",), 16),
        name="rope",
    )(pos)


_RET_LOG_GAMMA = [math.log1p(-(2.0 ** (-5.0 - h))) for h in range(RET_HEADS)]


def _ret_kernel(q_ref, k_ref, v_ref, g_ref, cos_ref, sin_ref, ng_ref, nb_ref, o_ref, state_ref):
    c = RET_CHUNK

    @pl.when(pl.program_id(0) == 0)
    def _():
        state_ref[...] = jnp.zeros_like(state_ref)

    cos = cos_ref[...]
    sin = sin_ref[...]
    ri = lax.broadcasted_iota(jnp.int32, (c, c), 0).astype(F32)
    ci = lax.broadcasted_iota(jnp.int32, (c, c), 1).astype(F32)
    rel = ri - ci
    scale = RET_DH ** -0.5
    for h in range(RET_HEADS):
        sl = slice(h * RET_DH, (h + 1) * RET_DH)
        lg = _RET_LOG_GAMMA[h]
        q = q_ref[:, sl]
        k = k_ref[:, sl]
        qr = q * cos + pltpu.roll(q, RET_DH // 2, 1) * sin
        kr = (k * cos + pltpu.roll(k, RET_DH // 2, 1) * sin) * scale
        v_bf = v_ref[:, sl].astype(BF16)
        decay = jnp.where(rel >= 0, jnp.exp(lg * jnp.maximum(rel, 0.0)), 0.0)
        q_bf = qr.astype(BF16)
        scores = _dot_nt(q_bf, kr.astype(BF16)) * decay
        inner = _dot(scores.astype(BF16), v_bf)
        state = state_ref[h]
        cross = _dot(q_bf, state.astype(BF16)) * jnp.exp(lg * (ri + 1.0))
        k_dec = (kr * jnp.exp(lg * (c - 1.0 - ri))).astype(BF16)
        state_ref[h] = math.exp(lg * c) * state + _dot_tn(k_dec, v_bf)
        o = inner + cross
        mu = jnp.mean(o, axis=-1, keepdims=True)
        d = o - mu
        var = jnp.mean(d * d, axis=-1, keepdims=True)
        on = d * lax.rsqrt(var + GN_EPS) * ng_ref[:, sl] + nb_ref[:, sl]
        g = g_ref[:, sl]
        o_ref[:, sl] = (g * jax.nn.sigmoid(g) * on).astype(BF16)


def _retention(proj, cos_t, sin_t, ng, nb, layer):
    ret = proj
    t = ret.shape[0]
    c = RET_CHUNK
    col = lambda j: pl.BlockSpec((c, W_BR), lambda i, j=j: (i, j + OFF_RET // W_BR))
    vec = pl.BlockSpec((None, 1, W_BR), lambda i: (layer, 0, 0))
    tab = pl.BlockSpec((c, RET_DH), lambda i: (i, 0))
    return pl.pallas_call(
        _ret_kernel,
        grid=(t // c,),
        in_specs=[col(0), col(1), col(2), col(3), tab, tab, vec, vec],
        out_specs=pl.BlockSpec((c, W_BR), lambda i: (i, 0)),
        out_shape=jax.ShapeDtypeStruct((t, W_BR), BF16),
        scratch_shapes=[pltpu.VMEM((RET_HEADS, RET_DH, RET_DH), F32)],
        compiler_params=_params(("arbitrary",), 32),
        name="retention",
    )(ret, ret, ret, ret, cos_t, sin_t, ng, nb)


def _seg64_sum(x):
    tile = 2 * RWKV_DH
    first = lax.broadcasted_iota(jnp.int32, (x.shape[0], tile), 1) < RWKV_DH
    outs = []
    for t in range(x.shape[1] // tile):
        xt = x[:, t * tile:(t + 1) * tile]
        s0 = jnp.sum(jnp.where(first, xt, 0.0), axis=-1, keepdims=True)
        s1 = jnp.sum(jnp.where(first, 0.0, xt), axis=-1, keepdims=True)
        outs.append(jnp.where(first, s0, s1))
    return jnp.concatenate(outs, axis=1)


def _cumsum_rows(x):
    rows = lax.broadcasted_iota(jnp.int32, x.shape, 0)
    s = 1
    while s < x.shape[0]:
        x = x + jnp.where(rows >= s, pltpu.roll(x, s, 0), 0.0)
        s *= 2
    return x


def _rwkv_kernel(r_ref, k_ref, v_ref, lo_ref, mu_ref, w0_ref, a0_ref, kk_ref, ka_ref, rk_ref, ng_ref, nb_ref,
                 w2_ref, a2_ref, g2_ref, o_ref, last_ref, s_ref):
    L = RWKV_CHUNK
    P = 2 * RWKV_DH
    nrows = r_ref.shape[0]

    @pl.when(pl.program_id(0) == 0)
    def _():
        last_ref[...] = jnp.zeros_like(last_ref)
        s_ref[...] = jnp.zeros_like(s_ref)

    def token_shift(z_ref, c0, c1):
        z = z_ref[...]
        rows = lax.broadcasted_iota(jnp.int32, z.shape, 0)
        zprev = jnp.where(rows == 0, last_ref[:, c0:c1], pltpu.roll(z, 1, 0))
        last_ref[:, c0:c1] = z[nrows - 1:, :]
        return z + mu_ref[:, c0:c1] * (zprev - z)

    r = token_shift(r_ref, 0, W_BR)
    k = token_shift(k_ref, W_BR, 2 * W_BR)
    v = token_shift(v_ref, 2 * W_BR, 3 * W_BR)
    lo = token_shift(lo_ref, 3 * W_BR, RWKV_IN)
    w_lo = lo[:, 0:64]
    a_lo = lo[:, 64:128]
    g_lo = lo[:, 128:256]

    wpre = w0_ref[...] + _dot(jnp.tanh(w_lo).astype(BF16), w2_ref[...])
    nw = -wpre
    softplus = jnp.maximum(nw, 0.0) + jnp.log1p(jnp.exp(-jnp.abs(nw)))
    w = -softplus - 0.5
    lw = -jnp.exp(w)
    a = jax.nn.sigmoid(a0_ref[...] + _dot(a_lo.astype(BF16), a2_ref[...]))
    g = _dot(jax.nn.sigmoid(g_lo).astype(BF16), g2_ref[...])

    kkv = k * kk_ref[...]
    kkn = kkv / jnp.maximum(jnp.sqrt(_seg64_sum(kkv * kkv)), 1e-12)
    k2 = k * (1.0 + (a - 1.0) * ka_ref[...])
    b = kkn * a

    chunks = range(nrows // L)
    rows_of = lambda x, c: x[c * L:(c + 1) * L, :]
    cum = jnp.concatenate([_cumsum_rows(rows_of(lw, c)) for c in chunks], axis=0)
    cum_last = jnp.concatenate(
        [jnp.broadcast_to(cum[(c + 1) * L - 1:(c + 1) * L, :], (L, W_BR)) for c in chunks], axis=0)
    e_to_end = jnp.exp(cum_last - cum)
    e_inv = jnp.exp(-cum)
    kkd = kkn * jnp.exp(cum - lw)
    rd = r * jnp.exp(cum)
    bd = b * e_inv
    kd = k2 * e_inv
    bd2 = b * e_to_end
    kd2 = k2 * e_to_end

    li = lax.broadcasted_iota(jnp.int32, (P, P), 0)
    lj = lax.broadcasted_iota(jnp.int32, (P, P), 1)
    tt = li % L
    ts = lj % L
    strict = tt > ts
    incl = tt >= ts
    eye = jnp.where(li == lj, 1.0, 0.0)
    first_half = lax.broadcasted_iota(jnp.int32, (L, P), 1) < RWKV_DH

    def stack(x, c, p):
        xp = x[c * L:(c + 1) * L, p * P:(p + 1) * P]
        return jnp.concatenate([jnp.where(first_half, xp, 0.0), jnp.where(first_half, 0.0, xp)],
                               axis=0).astype(BF16)

    pairs = range(RWKV_HEADS // 2)
    cps = [(c, p) for c in chunks for p in pairs]
    kkm = {cp: stack(kkd, *cp) for cp in cps}
    rm = {cp: stack(rd, *cp) for cp in cps}
    vm = {cp: stack(v, *cp) for cp in cps}
    gram = {cp: _dot_nt(jnp.concatenate([kkm[cp], rm[cp]], axis=0),
                        jnp.concatenate([stack(bd, *cp), stack(kd, *cp)], axis=0)) for cp in cps}
    a_bb = {cp: jnp.where(strict, gram[cp][:P, :P], 0.0) for cp in cps}
    av = {cp: _dot(jnp.where(strict, gram[cp][:P, P:], 0.0).astype(BF16), vm[cp]) for cp in cps}
    r_bk = {cp: jnp.concatenate([jnp.where(incl, gram[cp][P:, :P], 0.0), jnp.where(incl, gram[cp][P:, P:], 0.0)],
                                axis=1).astype(BF16) for cp in cps}
    lvl = (tt // 2) == (ts // 2)
    t_inv = {cp: eye - jnp.where(lvl, a_bb[cp], 0.0) for cp in cps}
    blk = 2
    while blk < L:
        lvl = ((tt // (2 * blk)) == (ts // (2 * blk))) & ((tt // blk) != (ts // blk))
        t_bf = {cp: t_inv[cp].astype(BF16) for cp in cps}
        mt = {cp: _dot(jnp.where(lvl, a_bb[cp], 0.0).astype(BF16), t_bf[cp]).astype(BF16) for cp in cps}
        t_inv = {cp: t_inv[cp] - _dot(t_bf[cp], mt[cp]) for cp in cps}
        blk *= 2
    t_bf = {cp: t_inv[cp].astype(BF16) for cp in cps}

    state = [s_ref[p] for p in pairs]
    y_rows = []
    for c in chunks:
        s_bf = [state[p].astype(BF16) for p in pairs]
        wm = [(_dot_nt(kkm[c, p], s_bf[p]) + av[c, p]).astype(BF16) for p in pairs]
        ys0 = [_dot_nt(rm[c, p], s_bf[p]) for p in pairs]
        bkm2 = [jnp.concatenate([stack(bd2, c, p), stack(kd2, c, p)], axis=0) for p in pairs]
        p_last = jnp.exp(cum[(c + 1) * L - 1:(c + 1) * L, :])
        uv = [jnp.concatenate([(-_dot(t_bf[c, p], wm[p])).astype(BF16), vm[c, p]], axis=0)
              for p in pairs]
        state = [state[p] * p_last[:, p * P:(p + 1) * P] + _dot_tn(uv[p], bkm2[p]) for p in pairs]
        ys = [ys0[p] + _dot(r_bk[c, p], uv[p]) for p in pairs]
        y_rows.append(jnp.concatenate([y[:L] + y[L:] for y in ys], axis=1))
    for p in pairs:
        s_ref[p] = state[p]
    y = jnp.concatenate(y_rows, axis=0)

    inv_n = 1.0 / RWKV_DH
    mu = _seg64_sum(y) * inv_n
    d = y - mu
    var = _seg64_sum(d * d) * inv_n
    yn = d * lax.rsqrt(var + RWKV_GN_EPS) * ng_ref[...] + nb_ref[...]
    bonus = _seg64_sum(r * k2 * rk_ref[...]) * v
    o_ref[...] = ((yn + bonus) * g).astype(BF16)


def _rwkv(proj, prm, layer):
    t = proj.shape[0]
    tr = min(RWKV_ROWS, t)
    lo_w = RWKV_IN - 3 * W_BR
    vec = lambda n: pl.BlockSpec((None, 1, n), lambda i: (layer, 0, 0))
    mat = lambda r: pl.BlockSpec((None, r, W_BR), lambda i: (layer, 0, 0))
    wide = lambda j: pl.BlockSpec((tr, W_BR), lambda i, j=j: (i, OFF_RWKV // W_BR + j))
    return pl.pallas_call(
        _rwkv_kernel,
        grid=(t // tr,),
        in_specs=[wide(0), wide(1), wide(2),
                  pl.BlockSpec((tr, lo_w), lambda i: (i, (OFF_RWKV + 3 * W_BR) // lo_w)), vec(RWKV_IN)]
                 + [vec(W_BR)] * 7 + [mat(64), mat(64), mat(128)],
        out_specs=pl.BlockSpec((tr, W_BR), lambda i: (i, 0)),
        out_shape=jax.ShapeDtypeStruct((t, W_BR), BF16),
        scratch_shapes=[pltpu.VMEM((1, RWKV_IN), F32),
                        pltpu.VMEM((RWKV_HEADS // 2, 2 * RWKV_DH, 2 * RWKV_DH), F32)],
        compiler_params=_params(("arbitrary",), 32),
        name="rwkv7",
    )(proj, proj, proj, proj, prm["mu"], prm["w0"], prm["a0"], prm["k_k"], prm["k_a"], prm["r_k"], prm["ng"],
      prm["nb"], prm["w2"], prm["a2"], prm["g2"])


def _gelu_tanh(x):
    c = math.sqrt(2.0 / math.pi)
    return x * (0.5 * (1.0 + jnp.tanh(c * (x + 0.044715 * (x * x * x)))))


def _s5_kernel(u_ref, lre_ref, lim_ref, ls_ref, bcat_ref, bswp_ref, ccat_ref, cswp_ref, d_ref, z_ref,
               et_ref, bb_ref, fbig_ref, wot_ref, mbig_ref, x_ref, prev_ref, a_ref):
    C = SSM_CHUNK
    GL = SSM_GROUP * SSM_GB
    PS = SSM_STATE
    half = SSM_GB * PS
    n = u_ref.shape[0] // C
    u_step = lambda j: u_ref[pl.ds(j, n, stride=C), :]

    @pl.when(pl.program_id(0) == 0)
    def _():
        fbig_ref[...] = jnp.zeros_like(fbig_ref)
        wot_ref[...] = jnp.zeros_like(wot_ref)
        mbig_ref[...] = jnp.zeros_like(mbig_ref)

    first1 = lax.broadcasted_iota(jnp.int32, (1, GL), 1) < PS
    sgn = jnp.where(first1, -1.0, 1.0)
    first16 = lax.broadcasted_iota(jnp.int32, (SSM_GROUP, GL), 1) < PS
    tau = lax.broadcasted_iota(jnp.int32, (24, GL), 0).astype(F32)

    for g in range(SSM_GB):
        lr = lre_ref[g:g + 1, :]
        li = lim_ref[g:g + 1, :]
        dt = jnp.exp(ls_ref[g:g + 1, :])
        mag = jnp.exp(lr * dt)
        ab_re = mag * jnp.cos(li * dt)
        ab_im = mag * jnp.sin(li * dt)
        denom = lr * lr + li * li
        f_re = ((ab_re - 1.0) * lr + ab_im * li) / denom
        f_im = (ab_im * lr - (ab_re - 1.0) * li) / denom
        bcat = bcat_ref[g]
        bswp = bswp_ref[g]
        bbs = f_re * bcat + (sgn * f_im) * bswp
        bbw = f_re * bswp - (sgn * f_im) * bcat
        pmag = jnp.exp(tau * (lr * dt))
        pw_re = pmag * jnp.cos(tau * (li * dt))
        pw_im = pmag * jnp.sin(tau * (li * dt))
        ccat = ccat_ref[g]
        cswp = cswp_ref[g]
        rows = slice(g * SSM_GROUP, (g + 1) * SSM_GROUP)
        tile = slice((g // 2) * GL, (g // 2 + 1) * GL)
        tile_im = slice(half + (g // 2) * GL, half + (g // 2 + 1) * GL)
        odd = g % 2

        def planes(xy, yx):
            if odd:
                return jnp.where(first16, 0.0, yx).astype(BF16), jnp.where(first16, 0.0, xy).astype(BF16)
            return jnp.where(first16, xy, 0.0).astype(BF16), jnp.where(first16, yx, 0.0).astype(BF16)

        for t in range(C + 1):
            pr = pw_re[t:t + 1, :]
            pi = pw_im[t:t + 1, :]
            e = pr * (-sgn * ccat) - pi * cswp
            et_ref[t, rows, :] = e
            if t >= 1:
                esw = pr * (sgn * cswp) - pi * ccat
                o_re, o_im = planes(e, esw)
                orow = slice((t - 1) * GL + g * SSM_GROUP, (t - 1) * GL + (g + 1) * SSM_GROUP)
                wot_ref[orow, tile] = o_re
                wot_ref[orow, tile_im] = o_im
            if t < C:
                qr = pw_re[C - 1 - t:C - t, :]
                qi = pw_im[C - 1 - t:C - t, :]
                f = qr * bbs + qi * (sgn * bbw)
                fsw = qr * bbw - qi * (sgn * bbs)
                i_re, i_im = planes(f, fsw)
                irow = slice(t * GL + g * SSM_GROUP, t * GL + (g + 1) * SSM_GROUP)
                fbig_ref[irow, tile] = i_re
                fbig_ref[irow, tile_im] = i_im
        bb_ref[rows, :] = bbs
        lanes = slice(odd * PS, (odd + 1) * PS)
        dst = slice((g // 2) * GL + odd * PS, (g // 2) * GL + (odd + 1) * PS)
        a_ref[0:1, dst] = pw_re[C:C + 1, lanes]
        a_ref[1:2, dst] = pw_im[C:C + 1, lanes]

    bi = lax.broadcasted_iota(jnp.int32, (GL, GL), 0) // SSM_GROUP
    bj = lax.broadcasted_iota(jnp.int32, (GL, GL), 1) // SSM_GROUP
    bh, bl = _split2(bb_ref[...])
    for t in range(C):
        eh, el = _split2(et_ref[t])
        kt = jnp.where(bi == bj, _dot_nt(bh, eh) + _dot_nt(bh, el) + _dot_nt(bl, eh), 0.0).astype(BF16)
        for j in range(C - t):
            mbig_ref[j * GL:(j + 1) * GL, (j + t) * GL:(j + t + 1) * GL] = kt

    ucat = jnp.concatenate([u_step(j).astype(BF16) for j in range(C)], axis=1)
    x_ref[...] = _dot(ucat, fbig_ref[...])

    a_re = a_ref[0:1, :]
    a_im = a_ref[1:2, :]

    def step(i, carry):
        s_re, s_im = carry
        prev_ref[pl.ds(i, 1), 0:half] = s_re
        prev_ref[pl.ds(i, 1), half:2 * half] = s_im
        x_re = x_ref[pl.ds(i, 1), 0:half]
        x_im = x_ref[pl.ds(i, 1), half:2 * half]
        return (a_re * s_re - a_im * s_im + x_re, a_re * s_im + a_im * s_re + x_im)

    zero = jnp.zeros((1, half), F32)
    lax.fori_loop(0, n, step, (zero, zero), unroll=8)

    prev_bf = prev_ref[...].astype(BF16)
    steps_per_dot = 4
    for q in range(C // steps_per_dot):
        kq = (q + 1) * steps_per_dot * GL
        cols = slice(q * steps_per_dot * GL, (q + 1) * steps_per_dot * GL)
        y = _dot(ucat[:, :kq], mbig_ref[0:kq, cols]) + _dot_nt(prev_bf, wot_ref[cols, :])
        for ii in range(steps_per_dot):
            i = q * steps_per_dot + ii
            yi = y[:, ii * GL:(ii + 1) * GL] + d_ref[...] * u_step(i)
            z_ref[i] = _gelu_tanh(yi).astype(BF16)


def _s5(proj, prm, layer):
    t = proj.shape[0]
    C = SSM_CHUNK
    n = t // C
    gl = SSM_GROUP * SSM_GB
    nblk = W_BR // gl
    st = 2 * SSM_GB * SSM_STATE
    lam = pl.BlockSpec((None, SSM_GB, gl), lambda b: (layer, b, 0))
    bc = pl.BlockSpec((None, SSM_GB, SSM_GROUP, gl), lambda b: (layer, b, 0, 0))
    return pl.pallas_call(
        _s5_kernel,
        grid=(nblk,),
        in_specs=[pl.BlockSpec((t, gl), lambda b: (0, b)), lam, lam, lam, bc, bc, bc, bc,
                  pl.BlockSpec((None, 1, gl), lambda b: (layer, 0, b))],
        out_specs=pl.BlockSpec((C, n, gl), lambda b: (0, 0, b)),
        out_shape=jax.ShapeDtypeStruct((C, n, W_BR), BF16),
        scratch_shapes=[pltpu.VMEM((C + 1, gl, gl), F32), pltpu.VMEM((gl, gl), F32),
                        pltpu.VMEM((C * gl, st), BF16), pltpu.VMEM((C * gl, st), BF16),
                        pltpu.VMEM((C * gl, C * gl), BF16),
                        pltpu.VMEM((n, st), F32), pltpu.VMEM((n, st), F32), pltpu.VMEM((8, st // 2), F32)],
        compiler_params=_params(("arbitrary",), 56),
        name="s5",
    )(proj, prm["lam_re"], prm["lam_im"], prm["log_step"], prm["b_cat"], prm["b_swp"], prm["c_cat"],
      prm["c_swp"], prm["d"])


def _merge_kernel(z_ref, r_ref, k_ref, wa_ref, wb_ref, wr_ref, wk_ref, g0_ref, g1_ref, g2_ref, o_ref,
                  wa_bf, wb_bf, wr_bf, wk_bf):
    for w_ref, w_bf in ((wa_ref, wa_bf), (wb_ref, wb_bf), (wr_ref, wr_bf), (wk_ref, wk_bf)):
        _cast_once(w_ref, w_bf)
    sub = min(SUB_ROWS, z_ref.shape[0])
    for s in range(z_ref.shape[0] // sub):
        rows = slice(s * sub, (s + 1) * sub)
        z = z_ref[rows, :]
        y_ssm = _dot(z, wa_bf[...]) * jax.nn.sigmoid(_dot(z, wb_bf[...]))
        y_ret = _dot(r_ref[rows, :], wr_bf[...])
        y_rwkv = _dot(k_ref[rows, :], wk_bf[...])
        gate = lambda g_ref: jax.nn.sigmoid(g_ref[rows, :])
        o_ref[rows, :] = (gate(g0_ref) * y_ssm + gate(g1_ref) * y_ret + gate(g2_ref) * y_rwkv).astype(BF16)


def _merge(z, o_ret, o_rwkv, w_glu, w_ret, w_rwkv, gates, layer):
    t = z.shape[0]
    tm = min(512, t)
    tn = 512
    nb = D_MODEL // tn
    act = pl.BlockSpec((tm, W_BR), lambda j, i: (i, 0))
    wsp = lambda off: pl.BlockSpec((None, W_BR, tn), lambda j, i, off=off: (layer, 0, j + off))
    gsp = lambda off: pl.BlockSpec((tm, tn), lambda j, i, off=off: (i, j + off))
    return pl.pallas_call(
        _merge_kernel,
        grid=(nb, t // tm),
        in_specs=[act, act, act, wsp(0), wsp(nb), wsp(0), wsp(0), gsp(0), gsp(nb), gsp(2 * nb)],
        out_specs=pl.BlockSpec((tm, tn), lambda j, i: (i, j)),
        out_shape=jax.ShapeDtypeStruct((t, D_MODEL), BF16),
        scratch_shapes=[pltpu.VMEM((W_BR, tn), BF16)] * 4,
        compiler_params=_params(("parallel", "arbitrary"), 48),
        name="merge",
    )(z, o_ret, o_rwkv, w_glu, w_glu, w_ret, w_rwkv, gates, gates, gates)


def _residual_ln_rows(a_ref, w_bf, x_ref, g_ref, b_ref, o_ref, obf_ref):
    sub = min(128, a_ref.shape[0])
    for s in range(a_ref.shape[0] // sub):
        rows = slice(s * sub, (s + 1) * sub)
        y = DEEPNORM_ALPHA * x_ref[rows, :] + _dot(a_ref[rows, :], w_bf[...])
        out = _layer_norm(y, g_ref[...], b_ref[...])
        o_ref[rows, :] = out
        obf_ref[rows, :] = out.astype(BF16)


def _out_ln_kernel(m_ref, w_ref, x_ref, g_ref, b_ref, o_ref, obf_ref, w_bf):
    @pl.when(pl.program_id(0) == 0)
    def _():
        w_bf[...] = w_ref[...].astype(BF16)

    _residual_ln_rows(m_ref, w_bf, x_ref, g_ref, b_ref, o_ref, obf_ref)


def _out_ln(m, w_all, x, g, b, layer):
    t = x.shape[0]
    tm = min(256, t)
    row = pl.BlockSpec((tm, D_MODEL), lambda i: (i, 0))
    vec = pl.BlockSpec((None, 1, D_MODEL), lambda i: (layer, 0, 0))
    whole = pl.BlockSpec((None, D_MODEL, D_MODEL), lambda i: (layer, 0, 0), pipeline_mode=pl.Buffered(1))
    return pl.pallas_call(
        _out_ln_kernel,
        grid=(t // tm,),
        in_specs=[row, whole, row, vec, vec],
        out_specs=[row, row],
        out_shape=[jax.ShapeDtypeStruct((t, D_MODEL), F32), jax.ShapeDtypeStruct((t, D_MODEL), BF16)],
        scratch_shapes=[pltpu.VMEM((D_MODEL, D_MODEL), BF16)],
        compiler_params=_params(("arbitrary",), 48),
        name="out_ln",
    )(m, w_all, x, g, b)


def _ffn_up_kernel(x_ref, w1_ref, w2_ref, c1_ref, c2_ref, o_ref, h1_ref, h2_ref, w1_bf, w2_bf):
    tm = x_ref.shape[0]
    _cast_once(w1_ref, w1_bf)
    _cast_once(w2_ref, w2_bf)

    @pl.when(pl.program_id(1) == 0)
    def _():
        h1_ref[0:8, :] = jnp.zeros((8, h1_ref.shape[1]), F32)
        h2_ref[0:8, :] = jnp.zeros((8, h2_ref.shape[1]), F32)

    sub = tm
    for s in range(tm // sub):
        r0 = s * sub
        x = x_ref[r0:r0 + sub, :]

        def conv(w_bf, c_ref, h_ref):
            h = _dot(x, w_bf[...])
            h_ref[8 + r0:8 + r0 + sub, :] = h
            return (c_ref[2:3, :] * h + c_ref[1:2, :] * h_ref[7 + r0:7 + r0 + sub, :]
                    + c_ref[0:1, :] * h_ref[6 + r0:6 + r0 + sub, :])

        a = conv(w1_bf, c1_ref, h1_ref)
        b = conv(w2_bf, c2_ref, h2_ref)
        o_ref[r0:r0 + sub, :] = (a * jax.nn.sigmoid(a) * b).astype(BF16)
    h1_ref[0:8, :] = h1_ref[tm:tm + 8, :]
    h2_ref[0:8, :] = h2_ref[tm:tm + 8, :]


def _ffn_up(x_bf, w_up, w_conv, layer):
    t = x_bf.shape[0]
    tm = min(1024, t)
    tn = 512
    nb = D_FF // tn
    wsp = lambda off: pl.BlockSpec((None, D_MODEL, tn), lambda j, i, off=off: (layer, 0, j + off))
    csp = lambda off: pl.BlockSpec((None, 3, tn), lambda j, i, off=off: (layer, 0, j + off))
    return pl.pallas_call(
        _ffn_up_kernel,
        grid=(nb, t // tm),
        in_specs=[pl.BlockSpec((tm, D_MODEL), lambda j, i: (i, 0)), wsp(0), wsp(nb), csp(0), csp(nb)],
        out_specs=pl.BlockSpec((tm, tn), lambda j, i: (i, j)),
        out_shape=jax.ShapeDtypeStruct((t, D_FF), BF16),
        scratch_shapes=[pltpu.VMEM((tm + 8, tn), F32), pltpu.VMEM((tm + 8, tn), F32),
                        pltpu.VMEM((D_MODEL, tn), BF16), pltpu.VMEM((D_MODEL, tn), BF16)],
        compiler_params=_params(("parallel", "arbitrary"), 48),
        name="ffn_up",
    )(x_bf, w_up, w_up, w_conv, w_conv)


def _ffn_down_kernel(a_ref, w_ref, x_ref, g_ref, b_ref, o_ref, obf_ref):
    _residual_ln_rows(a_ref, w_ref, x_ref, g_ref, b_ref, o_ref, obf_ref)


def _ffn_down(act, w_all, x, g, b, layer):
    t = x.shape[0]
    tm = min(256, t)
    row = pl.BlockSpec((tm, D_MODEL), lambda i: (i, 0))
    vec = pl.BlockSpec((None, 1, D_MODEL), lambda i: (layer, 0, 0))
    whole = pl.BlockSpec((None, D_FF, D_MODEL), lambda i: (layer, 0, 0), pipeline_mode=pl.Buffered(1))
    return pl.pallas_call(
        _ffn_down_kernel,
        grid=(t // tm,),
        in_specs=[pl.BlockSpec((tm, D_FF), lambda i: (i, 0)), whole, row, vec, vec],
        out_specs=[row, row],
        out_shape=[jax.ShapeDtypeStruct((t, D_MODEL), F32), jax.ShapeDtypeStruct((t, D_MODEL), BF16)],
        compiler_params=_params(("parallel",), 48),
        name="ffn_down",
    )(act, w_all, x, g, b)


def kernel(x, positions, w_in, ssm_lambda_re, ssm_lambda_im, ssm_log_step, ssm_b_re, ssm_b_im, ssm_c_re, ssm_c_im, ssm_d, ssm_glu, ret_norm_g, ret_norm_b, ret_out, rwkv_mu, rwkv_w0, rwkv_w2, rwkv_a0, rwkv_a2, rwkv_g2, rwkv_k_k, rwkv_k_a, rwkv_r_k, rwkv_norm_g, rwkv_norm_b, rwkv_out, w_o, ln1_g, ln1_b, ffn_up, ffn_conv, ffn_down, ln2_g, ln2_b):
    bsz, t, _ = x.shape
    assert bsz == 1
    depth = w_in.shape[0]
    vec3 = lambda p: p.reshape(depth, 1, p.shape[-1])
    down_bf = ffn_down.astype(BF16)

    twice = lambda a, b: jnp.concatenate([a, b], axis=-1)
    b_re_t, b_im_t = jnp.swapaxes(ssm_b_re, -1, -2), jnp.swapaxes(ssm_b_im, -1, -2)
    log_step = jnp.broadcast_to(ssm_log_step[..., None], ssm_lambda_re.shape)
    s5_prm = dict(
        lam_re=twice(ssm_lambda_re, ssm_lambda_re), lam_im=twice(ssm_lambda_im, ssm_lambda_im),
        log_step=twice(log_step, log_step),
        b_cat=twice(b_re_t, b_im_t), b_swp=twice(b_im_t, b_re_t),
        c_cat=twice(ssm_c_re, ssm_c_im), c_swp=twice(ssm_c_im, ssm_c_re),
        d=vec3(ssm_d))
    rwkv_prm = dict(mu=vec3(rwkv_mu), w0=vec3(rwkv_w0), a0=vec3(rwkv_a0), k_k=vec3(rwkv_k_k), k_a=vec3(rwkv_k_a),
                    r_k=vec3(rwkv_r_k), ng=vec3(rwkv_norm_g), nb=vec3(rwkv_norm_b),
                    w2=rwkv_w2.astype(BF16), a2=rwkv_a2.astype(BF16), g2=rwkv_g2.astype(BF16))
    ret_g, ret_b = vec3(ret_norm_g), vec3(ret_norm_b)
    ln1g, ln1b, ln2g, ln2b = vec3(ln1_g), vec3(ln1_b), vec3(ln2_g), vec3(ln2_b)

    cos_t, sin_t = _rope_tables(positions)
    xf = x.reshape(t, D_MODEL)
    xb = xf.astype(BF16)
    for l in range(depth):
        proj = _proj(xb, w_in, l, 0, OFF_GATE, 768)
        gates = _proj(xb, w_in, l, OFF_GATE, N_GATE, 768)
        z = _s5(proj, s5_prm, l).transpose(1, 0, 2).reshape(t, W_BR)
        o_ret = _retention(proj, cos_t, sin_t, ret_g, ret_b, l)
        o_rwkv = _rwkv(proj, rwkv_prm, l)
        merged = _merge(z, o_ret, o_rwkv, ssm_glu, ret_out, rwkv_out, gates, l)
        xf, xb = _out_ln(merged, w_o, xf, ln1g, ln1b, l)
        act = _ffn_up(xb, ffn_up, ffn_conv, l)
        xf, xb = _ffn_down(act, down_bf, xf, ln2g, ln2b, l)
    return xf.reshape(bsz, t, D_MODEL)
```

```python
import functools
import math

import jax
import jax.numpy as jnp
from jax import lax
from jax.experimental import pallas as pl
from jax.experimental.pallas import tpu as pltpu

F32 = jnp.float32
BF16 = jnp.bfloat16

D_MODEL = 2048
DEPTH = 4
W_BR = 1024
SSM_GROUP = 16
SSM_GROUPS = 64
SSM_STATE = 64
SSM_CHUNK = 16
SSM_GB = 8
RET_HEADS = 8
RET_DH = 128
RET_CHUNK = 128
ROPE_BASE = 10000.0
RWKV_HEADS = 16
RWKV_DH = 64
RWKV_CHUNK = 64
RWKV_ROWS = 256
RWKV_IN = 3328
D_FF = 5632
OFF_RET = 1024
OFF_RWKV = 5120
OFF_GATE = 8448
N_GATE = 3 * D_MODEL
DEEPNORM_ALPHA = (2.0 * DEPTH) ** 0.25
LN_EPS = 1e-5
GN_EPS = 1e-5
RWKV_GN_EPS = 64e-5
MIB = 1024 * 1024
SUB_ROWS = 256


def _params(sem, vmem_mib):
    return pltpu.CompilerParams(dimension_semantics=sem, vmem_limit_bytes=vmem_mib * MIB)


def _dot(a, b):
    return jnp.dot(a, b, preferred_element_type=F32)


def _dot_nt(a, b):
    return lax.dot_general(a, b, (((1,), (1,)), ((), ())), preferred_element_type=F32)


def _dot_tn(a, b):
    return lax.dot_general(a, b, (((0,), (0,)), ((), ())), preferred_element_type=F32)


def _split2(x):
    hi = x.astype(BF16)
    lo = (x - hi.astype(F32)).astype(BF16)
    return hi, lo


def _split3(x):
    hi = x.astype(BF16)
    r1 = x - hi.astype(F32)
    mid = r1.astype(BF16)
    lo = (r1 - mid.astype(F32)).astype(BF16)
    return hi, mid, lo


def _layer_norm(y, g, b):
    mu = jnp.mean(y, axis=-1, keepdims=True)
    d = y - mu
    var = jnp.mean(d * d, axis=-1, keepdims=True)
    return d * lax.rsqrt(var + LN_EPS) * g + b


def _cast_once(w_ref, wbf_ref):
    @pl.when(pl.program_id(1) == 0)
    def _():
        wbf_ref[...] = w_ref[...].astype(BF16)


def _proj_kernel(x_ref, w_ref, o_ref, wbf_ref):
    _cast_once(w_ref, wbf_ref)
    o_ref[...] = _dot(x_ref[...], wbf_ref[...])


def _proj(x_bf, w_all, layer, col0, ncols, tn):
    t, k = x_bf.shape
    tm = min(2048, t)
    off = col0 // tn
    assert off * tn == col0 and ncols % tn == 0
    return pl.pallas_call(
        _proj_kernel,
        grid=(ncols // tn, t // tm),
        in_specs=[pl.BlockSpec((tm, k), lambda j, i: (i, 0)),
                  pl.BlockSpec((None, k, tn), lambda j, i: (layer, 0, j + off))],
        out_specs=pl.BlockSpec((tm, tn), lambda j, i: (i, j)),
        out_shape=jax.ShapeDtypeStruct((t, ncols), F32),
        scratch_shapes=[pltpu.VMEM((k, tn), BF16)],
        compiler_params=_params(("parallel", "arbitrary"), 56),
        name="proj",
    )(x_bf, w_all)


def _rope_kernel(pos_ref, cos_ref, sin_ref):
    half = RET_DH // 2
    pos = pos_ref[...].astype(F32)
    lane = lax.broadcasted_iota(jnp.int32, (1, RET_DH), 1)
    idx = jnp.where(lane < half, lane, lane - half).astype(F32)
    inv_freq = jnp.exp(-(idx / half) * math.log(ROPE_BASE))
    ang = pos * inv_freq
    cos_ref[...] = jnp.cos(ang)
    s = jnp.sin(ang)
    sin_ref[...] = jnp.where(lane < half, -s, s)


def _rope_tables(positions):
    t = positions.shape[-1]
    tm = min(1024, t)
    pos = positions.reshape(t, 1)
    return pl.pallas_call(
        _rope_kernel,
        grid=(t // tm,),
        in_specs=[pl.BlockSpec((tm, 1), lambda i: (i, 0))],
        out_specs=[pl.BlockSpec((tm, RET_DH), lambda i: (i, 0))] * 2,
        out_shape=[jax.ShapeDtypeStruct((t, RET_DH), F32)] * 2,
        compiler_params=_params(("parallel",), 16),
        name="rope",
    )(pos)


_RET_LOG_GAMMA = [math.log1p(-(2.0 ** (-5.0 - h))) for h in range(RET_HEADS)]


def _ret_kernel(q_ref, k_ref, v_ref, g_ref, cos_ref, sin_ref, ng_ref, nb_ref, o_ref, state_ref):
    c = RET_CHUNK

    @pl.when(pl.program_id(0) == 0)
    def _():
        state_ref[...] = jnp.zeros_like(state_ref)

    cos = cos_ref[...]
    sin = sin_ref[...]
    ri = lax.broadcasted_iota(jnp.int32, (c, c), 0).astype(F32)
    ci = lax.broadcasted_iota(jnp.int32, (c, c), 1).astype(F32)
    rel = ri - ci
    scale = RET_DH ** -0.5
    for h in range(RET_HEADS):
        sl = slice(h * RET_DH, (h + 1) * RET_DH)
        lg = _RET_LOG_GAMMA[h]
        q = q_ref[:, sl]
        k = k_ref[:, sl]
        qr = q * cos + pltpu.roll(q, RET_DH // 2, 1) * sin
        kr = (k * cos + pltpu.roll(k, RET_DH // 2, 1) * sin) * scale
        v_bf = v_ref[:, sl].astype(BF16)
        decay = jnp.where(rel >= 0, jnp.exp(lg * jnp.maximum(rel, 0.0)), 0.0)
        q_bf = qr.astype(BF16)
        scores = _dot_nt(q_bf, kr.astype(BF16)) * decay
        inner = _dot(scores.astype(BF16), v_bf)
        state = state_ref[h]
        cross = _dot(q_bf, state.astype(BF16)) * jnp.exp(lg * (ri + 1.0))
        k_dec = (kr * jnp.exp(lg * (c - 1.0 - ri))).astype(BF16)
        state_ref[h] = math.exp(lg * c) * state + _dot_tn(k_dec, v_bf)
        o = inner + cross
        mu = jnp.mean(o, axis=-1, keepdims=True)
        d = o - mu
        var = jnp.mean(d * d, axis=-1, keepdims=True)
        on = d * lax.rsqrt(var + GN_EPS) * ng_ref[:, sl] + nb_ref[:, sl]
        g = g_ref[:, sl]
        o_ref[:, sl] = (g * jax.nn.sigmoid(g) * on).astype(BF16)


def _retention(proj, cos_t, sin_t, ng, nb, layer):
    ret = proj
    t = ret.shape[0]
    c = RET_CHUNK
    col = lambda j: pl.BlockSpec((c, W_BR), lambda i, j=j: (i, j + OFF_RET // W_BR))
    vec = pl.BlockSpec((None, 1, W_BR), lambda i: (layer, 0, 0))
    tab = pl.BlockSpec((c, RET_DH), lambda i: (i, 0))
    return pl.pallas_call(
        _ret_kernel,
        grid=(t // c,),
        in_specs=[col(0), col(1), col(2), col(3), tab, tab, vec, vec],
        out_specs=pl.BlockSpec((c, W_BR), lambda i: (i, 0)),
        out_shape=jax.ShapeDtypeStruct((t, W_BR), BF16),
        scratch_shapes=[pltpu.VMEM((RET_HEADS, RET_DH, RET_DH), F32)],
        compiler_params=_params(("arbitrary",), 32),
        name="retention",
    )(ret, ret, ret, ret, cos_t, sin_t, ng, nb)


def _seg64_sum(x):
    tile = 2 * RWKV_DH
    first = lax.broadcasted_iota(jnp.int32, (x.shape[0], tile), 1) < RWKV_DH
    outs = []
    for t in range(x.shape[1] // tile):
        xt = x[:, t * tile:(t + 1) * tile]
        s0 = jnp.sum(jnp.where(first, xt, 0.0), axis=-1, keepdims=True)
        s1 = jnp.sum(jnp.where(first, 0.0, xt), axis=-1, keepdims=True)
        outs.append(jnp.where(first, s0, s1))
    return jnp.concatenate(outs, axis=1)


def _cumsum_rows(x):
    rows = lax.broadcasted_iota(jnp.int32, x.shape, 0)
    s = 1
    while s < x.shape[0]:
        x = x + jnp.where(rows >= s, pltpu.roll(x, s, 0), 0.0)
        s *= 2
    return x


def _rwkv_kernel(r_ref, k_ref, v_ref, lo_ref, mu_ref, w0_ref, a0_ref, kk_ref, ka_ref, rk_ref, ng_ref, nb_ref,
                 w2_ref, a2_ref, g2_ref, o_ref, last_ref, s_ref):
    L = RWKV_CHUNK
    P = 2 * RWKV_DH
    nrows = r_ref.shape[0]

    @pl.when(pl.program_id(0) == 0)
    def _():
        last_ref[...] = jnp.zeros_like(last_ref)
        s_ref[...] = jnp.zeros_like(s_ref)

    def token_shift(z_ref, c0, c1):
        z = z_ref[...]
        rows = lax.broadcasted_iota(jnp.int32, z.shape, 0)
        zprev = jnp.where(rows == 0, last_ref[:, c0:c1], pltpu.roll(z, 1, 0))
        last_ref[:, c0:c1] = z[nrows - 1:, :]
        return z + mu_ref[:, c0:c1] * (zprev - z)

    r = token_shift(r_ref, 0, W_BR)
    k = token_shift(k_ref, W_BR, 2 * W_BR)
    v = token_shift(v_ref, 2 * W_BR, 3 * W_BR)
    lo = token_shift(lo_ref, 3 * W_BR, RWKV_IN)
    w_lo = lo[:, 0:64]
    a_lo = lo[:, 64:128]
    g_lo = lo[:, 128:256]

    wpre = w0_ref[...] + _dot(jnp.tanh(w_lo).astype(BF16), w2_ref[...])
    nw = -wpre
    softplus = jnp.maximum(nw, 0.0) + jnp.log1p(jnp.exp(-jnp.abs(nw)))
    w = -softplus - 0.5
    lw = -jnp.exp(w)
    a = jax.nn.sigmoid(a0_ref[...] + _dot(a_lo.astype(BF16), a2_ref[...]))
    g = _dot(jax.nn.sigmoid(g_lo).astype(BF16), g2_ref[...])

    kkv = k * kk_ref[...]
    kkn = kkv / jnp.maximum(jnp.sqrt(_seg64_sum(kkv * kkv)), 1e-12)
    k2 = k * (1.0 + (a - 1.0) * ka_ref[...])
    b = kkn * a

    chunks = range(nrows // L)
    rows_of = lambda x, c: x[c * L:(c + 1) * L, :]
    cum = jnp.concatenate([_cumsum_rows(rows_of(lw, c)) for c in chunks], axis=0)
    cum_last = jnp.concatenate(
        [jnp.broadcast_to(cum[(c + 1) * L - 1:(c + 1) * L, :], (L, W_BR)) for c in chunks], axis=0)
    e_to_end = jnp.exp(cum_last - cum)
    e_inv = jnp.exp(-cum)
    kkd = kkn * jnp.exp(cum - lw)
    rd = r * jnp.exp(cum)
    bd = b * e_inv
    kd = k2 * e_inv
    bd2 = b * e_to_end
    kd2 = k2 * e_to_end

    li = lax.broadcasted_iota(jnp.int32, (P, P), 0)
    lj = lax.broadcasted_iota(jnp.int32, (P, P), 1)
    tt = li % L
    ts = lj % L
    strict = tt > ts
    incl = tt >= ts
    eye = jnp.where(li == lj, 1.0, 0.0)
    first_half = lax.broadcasted_iota(jnp.int32, (L, P), 1) < RWKV_DH

    def stack(x, c, p):
        xp = x[c * L:(c + 1) * L, p * P:(p + 1) * P]
        return jnp.concatenate([jnp.where(first_half, xp, 0.0), jnp.where(first_half, 0.0, xp)],
                               axis=0).astype(BF16)

    pairs = range(RWKV_HEADS // 2)
    cps = [(c, p) for c in chunks for p in pairs]
    kkm = {cp: stack(kkd, *cp) for cp in cps}
    rm = {cp: stack(rd, *cp) for cp in cps}
    vm = {cp: stack(v, *cp) for cp in cps}
    gram = {cp: _dot_nt(jnp.concatenate([kkm[cp], rm[cp]], axis=0),
                        jnp.concatenate([stack(bd, *cp), stack(kd, *cp)], axis=0)) for cp in cps}
    a_bb = {cp: jnp.where(strict, gram[cp][:P, :P], 0.0) for cp in cps}
    av = {cp: _dot(jnp.where(strict, gram[cp][:P, P:], 0.0).astype(BF16), vm[cp]) for cp in cps}
    r_bk = {cp: jnp.concatenate([jnp.where(incl, gram[cp][P:, :P], 0.0), jnp.where(incl, gram[cp][P:, P:], 0.0)],
                                axis=1).astype(BF16) for cp in cps}
    lvl = (tt // 2) == (ts // 2)
    t_inv = {cp: eye - jnp.where(lvl, a_bb[cp], 0.0) for cp in cps}
    blk = 2
    while blk < L:
        lvl = ((tt // (2 * blk)) == (ts // (2 * blk))) & ((tt // blk) != (ts // blk))
        t_bf = {cp: t_inv[cp].astype(BF16) for cp in cps}
        mt = {cp: _dot(jnp.where(lvl, a_bb[cp], 0.0).astype(BF16), t_bf[cp]).astype(BF16) for cp in cps}
        t_inv = {cp: t_inv[cp] - _dot(t_bf[cp], mt[cp]) for cp in cps}
        blk *= 2
    t_bf = {cp: t_inv[cp].astype(BF16) for cp in cps}

    state = [s_ref[p] for p in pairs]
    y_rows = []
    for c in chunks:
        s_bf = [state[p].astype(BF16) for p in pairs]
        wm = [(_dot_nt(kkm[c, p], s_bf[p]) + av[c, p]).astype(BF16) for p in pairs]
        ys0 = [_dot_nt(rm[c, p], s_bf[p]) for p in pairs]
        bkm2 = [jnp.concatenate([stack(bd2, c, p), stack(kd2, c, p)], axis=0) for p in pairs]
        p_last = jnp.exp(cum[(c + 1) * L - 1:(c + 1) * L, :])
        uv = [jnp.concatenate([(-_dot(t_bf[c, p], wm[p])).astype(BF16), vm[c, p]], axis=0)
              for p in pairs]
        state = [state[p] * p_last[:, p * P:(p + 1) * P] + _dot_tn(uv[p], bkm2[p]) for p in pairs]
        ys = [ys0[p] + _dot(r_bk[c, p], uv[p]) for p in pairs]
        y_rows.append(jnp.concatenate([y[:L] + y[L:] for y in ys], axis=1))
    for p in pairs:
        s_ref[p] = state[p]
    y = jnp.concatenate(y_rows, axis=0)

    inv_n = 1.0 / RWKV_DH
    mu = _seg64_sum(y) * inv_n
    d = y - mu
    var = _seg64_sum(d * d) * inv_n
    yn = d * lax.rsqrt(var + RWKV_GN_EPS) * ng_ref[...] + nb_ref[...]
    bonus = _seg64_sum(r * k2 * rk_ref[...]) * v
    o_ref[...] = ((yn + bonus) * g).astype(BF16)


def _rwkv(proj, prm, layer):
    t = proj.shape[0]
    tr = min(RWKV_ROWS, t)
    lo_w = RWKV_IN - 3 * W_BR
    vec = lambda n: pl.BlockSpec((None, 1, n), lambda i: (layer, 0, 0))
    mat = lambda r: pl.BlockSpec((None, r, W_BR), lambda i: (layer, 0, 0))
    wide = lambda j: pl.BlockSpec((tr, W_BR), lambda i, j=j: (i, OFF_RWKV // W_BR + j))
    return pl.pallas_call(
        _rwkv_kernel,
        grid=(t // tr,),
        in_specs=[wide(0), wide(1), wide(2),
                  pl.BlockSpec((tr, lo_w), lambda i: (i, (OFF_RWKV + 3 * W_BR) // lo_w)), vec(RWKV_IN)]
                 + [vec(W_BR)] * 7 + [mat(64), mat(64), mat(128)],
        out_specs=pl.BlockSpec((tr, W_BR), lambda i: (i, 0)),
        out_shape=jax.ShapeDtypeStruct((t, W_BR), BF16),
        scratch_shapes=[pltpu.VMEM((1, RWKV_IN), F32),
                        pltpu.VMEM((RWKV_HEADS // 2, 2 * RWKV_DH, 2 * RWKV_DH), F32)],
        compiler_params=_params(("arbitrary",), 32),
        name="rwkv7",
    )(proj, proj, proj, proj, prm["mu"], prm["w0"], prm["a0"], prm["k_k"], prm["k_a"], prm["r_k"], prm["ng"],
      prm["nb"], prm["w2"], prm["a2"], prm["g2"])


def _gelu_tanh(x):
    c = math.sqrt(2.0 / math.pi)
    return x * (0.5 * (1.0 + jnp.tanh(c * (x + 0.044715 * (x * x * x)))))


def _s5_kernel(u_ref, lre_ref, lim_ref, ls_ref, bcat_ref, bswp_ref, ccat_ref, cswp_ref, d_ref, z_ref,
               et_ref, bb_ref, fbig_ref, wot_ref, mbig_ref, x_ref, prev_ref, a_ref):
    C = SSM_CHUNK
    GL = SSM_GROUP * SSM_GB
    PS = SSM_STATE
    half = SSM_GB * PS
    n = u_ref.shape[0] // C
    u_step = lambda j: u_ref[pl.ds(j, n, stride=C), :]

    @pl.when(pl.program_id(0) == 0)
    def _():
        fbig_ref[...] = jnp.zeros_like(fbig_ref)
        wot_ref[...] = jnp.zeros_like(wot_ref)
        mbig_ref[...] = jnp.zeros_like(mbig_ref)

    first1 = lax.broadcasted_iota(jnp.int32, (1, GL), 1) < PS
    sgn = jnp.where(first1, -1.0, 1.0)
    first16 = lax.broadcasted_iota(jnp.int32, (SSM_GROUP, GL), 1) < PS
    tau = lax.broadcasted_iota(jnp.int32, (24, GL), 0).astype(F32)

    for g in range(SSM_GB):
        lr = lre_ref[g:g + 1, :]
        li = lim_ref[g:g + 1, :]
        dt = jnp.exp(ls_ref[g:g + 1, :])
        mag = jnp.exp(lr * dt)
        ab_re = mag * jnp.cos(li * dt)
        ab_im = mag * jnp.sin(li * dt)
        denom = lr * lr + li * li
        f_re = ((ab_re - 1.0) * lr + ab_im * li) / denom
        f_im = (ab_im * lr - (ab_re - 1.0) * li) / denom
        bcat = bcat_ref[g]
        bswp = bswp_ref[g]
        bbs = f_re * bcat + (sgn * f_im) * bswp
        bbw = f_re * bswp - (sgn * f_im) * bcat
        pmag = jnp.exp(tau * (lr * dt))
        pw_re = pmag * jnp.cos(tau * (li * dt))
        pw_im = pmag * jnp.sin(tau * (li * dt))
        ccat = ccat_ref[g]
        cswp = cswp_ref[g]
        rows = slice(g * SSM_GROUP, (g + 1) * SSM_GROUP)
        tile = slice((g // 2) * GL, (g // 2 + 1) * GL)
        tile_im = slice(half + (g // 2) * GL, half + (g // 2 + 1) * GL)
        odd = g % 2

        def planes(xy, yx):
            if odd:
                return jnp.where(first16, 0.0, yx).astype(BF16), jnp.where(first16, 0.0, xy).astype(BF16)
            return jnp.where(first16, xy, 0.0).astype(BF16), jnp.where(first16, yx, 0.0).astype(BF16)

        for t in range(C + 1):
            pr = pw_re[t:t + 1, :]
            pi = pw_im[t:t + 1, :]
            e = pr * (-sgn * ccat) - pi * cswp
            et_ref[t, rows, :] = e
            if t >= 1:
                esw = pr * (sgn * cswp) - pi * ccat
                o_re, o_im = planes(e, esw)
                orow = slice((t - 1) * GL + g * SSM_GROUP, (t - 1) * GL + (g + 1) * SSM_GROUP)
                wot_ref[orow, tile] = o_re
                wot_ref[orow, tile_im] = o_im
            if t < C:
                qr = pw_re[C - 1 - t:C - t, :]
                qi = pw_im[C - 1 - t:C - t, :]
                f = qr * bbs + qi * (sgn * bbw)
                fsw = qr * bbw - qi * (sgn * bbs)
                i_re, i_im = planes(f, fsw)
                irow = slice(t * GL + g * SSM_GROUP, t * GL + (g + 1) * SSM_GROUP)
                fbig_ref[irow, tile] = i_re
                fbig_ref[irow, tile_im] = i_im
        bb_ref[rows, :] = bbs
        lanes = slice(odd * PS, (odd + 1) * PS)
        dst = slice((g // 2) * GL + odd * PS, (g // 2) * GL + (odd + 1) * PS)
        a_ref[0:1, dst] = pw_re[C:C + 1, lanes]
        a_ref[1:2, dst] = pw_im[C:C + 1, lanes]

    bi = lax.broadcasted_iota(jnp.int32, (GL, GL), 0) // SSM_GROUP
    bj = lax.broadcasted_iota(jnp.int32, (GL, GL), 1) // SSM_GROUP
    bh, bl = _split2(bb_ref[...])
    for t in range(C):
        eh, el = _split2(et_ref[t])
        kt = jnp.where(bi == bj, _dot_nt(bh, eh) + _dot_nt(bh, el) + _dot_nt(bl, eh), 0.0).astype(BF16)
        for j in range(C - t):
            mbig_ref[j * GL:(j + 1) * GL, (j + t) * GL:(j + t + 1) * GL] = kt

    ucat = jnp.concatenate([u_step(j).astype(BF16) for j in range(C)], axis=1)
    x_ref[...] = _dot(ucat, fbig_ref[...])

    a_re = a_ref[0:1, :]
    a_im = a_ref[1:2, :]

    def step(i, carry):
        s_re, s_im = carry
        prev_ref[pl.ds(i, 1), 0:half] = s_re
        prev_ref[pl.ds(i, 1), half:2 * half] = s_im
        x_re = x_ref[pl.ds(i, 1), 0:half]
        x_im = x_ref[pl.ds(i, 1), half:2 * half]
        return (a_re * s_re - a_im * s_im + x_re, a_re * s_im + a_im * s_re + x_im)

    zero = jnp.zeros((1, half), F32)
    lax.fori_loop(0, n, step, (zero, zero), unroll=8)

    prev_bf = prev_ref[...].astype(BF16)
    steps_per_dot = 4
    for q in range(C // steps_per_dot):
        kq = (q + 1) * steps_per_dot * GL
        cols = slice(q * steps_per_dot * GL, (q + 1) * steps_per_dot * GL)
        y = _dot(ucat[:, :kq], mbig_ref[0:kq, cols]) + _dot_nt(prev_bf, wot_ref[cols, :])
        for ii in range(steps_per_dot):
            i = q * steps_per_dot + ii
            yi = y[:, ii * GL:(ii + 1) * GL] + d_ref[...] * u_step(i)
            z_ref[i] = _gelu_tanh(yi).astype(BF16)


def _s5(proj, prm, layer):
    t = proj.shape[0]
    C = SSM_CHUNK
    n = t // C
    gl = SSM_GROUP * SSM_GB
    nblk = W_BR // gl
    st = 2 * SSM_GB * SSM_STATE
    lam = pl.BlockSpec((None, SSM_GB, gl), lambda b: (layer, b, 0))
    bc = pl.BlockSpec((None, SSM_GB, SSM_GROUP, gl), lambda b: (layer, b, 0, 0))
    return pl.pallas_call(
        _s5_kernel,
        grid=(nblk,),
        in_specs=[pl.BlockSpec((t, gl), lambda b: (0, b)), lam, lam, lam, bc, bc, bc, bc,
                  pl.BlockSpec((None, 1, gl), lambda b: (layer, 0, b))],
        out_specs=pl.BlockSpec((C, n, gl), lambda b: (0, 0, b)),
        out_shape=jax.ShapeDtypeStruct((C, n, W_BR), BF16),
        scratch_shapes=[pltpu.VMEM((C + 1, gl, gl), F32), pltpu.VMEM((gl, gl), F32),
                        pltpu.VMEM((C * gl, st), BF16), pltpu.VMEM((C * gl, st), BF16),
                        pltpu.VMEM((C * gl, C * gl), BF16),
                        pltpu.VMEM((n, st), F32), pltpu.VMEM((n, st), F32), pltpu.VMEM((8, st // 2), F32)],
        compiler_params=_params(("arbitrary",), 56),
        name="s5",
    )(proj, prm["lam_re"], prm["lam_im"], prm["log_step"], prm["b_cat"], prm["b_swp"], prm["c_cat"],
      prm["c_swp"], prm["d"])


def _merge_kernel(z_ref, r_ref, k_ref, wa_ref, wb_ref, wr_ref, wk_ref, g0_ref, g1_ref, g2_ref, o_ref,
                  wa_bf, wb_bf, wr_bf, wk_bf):
    for w_ref, w_bf in ((wa_ref, wa_bf), (wb_ref, wb_bf), (wr_ref, wr_bf), (wk_ref, wk_bf)):
        _cast_once(w_ref, w_bf)
    sub = min(SUB_ROWS, z_ref.shape[0])
    for s in range(z_ref.shape[0] // sub):
        rows = slice(s * sub, (s + 1) * sub)
        z = z_ref[rows, :]
        y_ssm = _dot(z, wa_bf[...]) * jax.nn.sigmoid(_dot(z, wb_bf[...]))
        y_ret = _dot(r_ref[rows, :], wr_bf[...])
        y_rwkv = _dot(k_ref[rows, :], wk_bf[...])
        gate = lambda g_ref: jax.nn.sigmoid(g_ref[rows, :])
        o_ref[rows, :] = (gate(g0_ref) * y_ssm + gate(g1_ref) * y_ret + gate(g2_ref) * y_rwkv).astype(BF16)


def _merge(z, o_ret, o_rwkv, w_glu, w_ret, w_rwkv, gates, layer):
    t = z.shape[0]
    tm = min(512, t)
    tn = 512
    nb = D_MODEL // tn
    act = pl.BlockSpec((tm, W_BR), lambda j, i: (i, 0))
    wsp = lambda off: pl.BlockSpec((None, W_BR, tn), lambda j, i, off=off: (layer, 0, j + off))
    gsp = lambda off: pl.BlockSpec((tm, tn), lambda j, i, off=off: (i, j + off))
    return pl.pallas_call(
        _merge_kernel,
        grid=(nb, t // tm),
        in_specs=[act, act, act, wsp(0), wsp(nb), wsp(0), wsp(0), gsp(0), gsp(nb), gsp(2 * nb)],
        out_specs=pl.BlockSpec((tm, tn), lambda j, i: (i, j)),
        out_shape=jax.ShapeDtypeStruct((t, D_MODEL), BF16),
        scratch_shapes=[pltpu.VMEM((W_BR, tn), BF16)] * 4,
        compiler_params=_params(("parallel", "arbitrary"), 48),
        name="merge",
    )(z, o_ret, o_rwkv, w_glu, w_glu, w_ret, w_rwkv, gates, gates, gates)


def _residual_ln_rows(a_ref, w_bf, x_ref, g_ref, b_ref, o_ref, obf_ref):
    sub = min(128, a_ref.shape[0])
    for s in range(a_ref.shape[0] // sub):
        rows = slice(s * sub, (s + 1) * sub)
        y = DEEPNORM_ALPHA * x_ref[rows, :] + _dot(a_ref[rows, :], w_bf[...])
        out = _layer_norm(y, g_ref[...], b_ref[...])
        o_ref[rows, :] = out
        obf_ref[rows, :] = out.astype(BF16)


def _out_ln_kernel(m_ref, w_ref, x_ref, g_ref, b_ref, o_ref, obf_ref, w_bf):
    @pl.when(pl.program_id(0) == 0)
    def _():
        w_bf[...] = w_ref[...].astype(BF16)

    _residual_ln_rows(m_ref, w_bf, x_ref, g_ref, b_ref, o_ref, obf_ref)


def _out_ln(m, w_all, x, g, b, layer):
    t = x.shape[0]
    tm = min(256, t)
    row = pl.BlockSpec((tm, D_MODEL), lambda i: (i, 0))
    vec = pl.BlockSpec((None, 1, D_MODEL), lambda i: (layer, 0, 0))
    whole = pl.BlockSpec((None, D_MODEL, D_MODEL), lambda i: (layer, 0, 0), pipeline_mode=pl.Buffered(1))
    return pl.pallas_call(
        _out_ln_kernel,
        grid=(t // tm,),
        in_specs=[row, whole, row, vec, vec],
        out_specs=[row, row],
        out_shape=[jax.ShapeDtypeStruct((t, D_MODEL), F32), jax.ShapeDtypeStruct((t, D_MODEL), BF16)],
        scratch_shapes=[pltpu.VMEM((D_MODEL, D_MODEL), BF16)],
        compiler_params=_params(("arbitrary",), 48),
        name="out_ln",
    )(m, w_all, x, g, b)


def _ffn_up_kernel(x_ref, w1_ref, w2_ref, c1_ref, c2_ref, o_ref, h1_ref, h2_ref, w1_bf, w2_bf):
    tm = x_ref.shape[0]
    _cast_once(w1_ref, w1_bf)
    _cast_once(w2_ref, w2_bf)

    @pl.when(pl.program_id(1) == 0)
    def _():
        h1_ref[0:8, :] = jnp.zeros((8, h1_ref.shape[1]), F32)
        h2_ref[0:8, :] = jnp.zeros((8, h2_ref.shape[1]), F32)

    sub = tm
    for s in range(tm // sub):
        r0 = s * sub
        x = x_ref[r0:r0 + sub, :]

        def conv(w_bf, c_ref, h_ref):
            h = _dot(x, w_bf[...])
            h_ref[8 + r0:8 + r0 + sub, :] = h
            return (c_ref[2:3, :] * h + c_ref[1:2, :] * h_ref[7 + r0:7 + r0 + sub, :]
                    + c_ref[0:1, :] * h_ref[6 + r0:6 + r0 + sub, :])

        a = conv(w1_bf, c1_ref, h1_ref)
        b = conv(w2_bf, c2_ref, h2_ref)
        o_ref[r0:r0 + sub, :] = (a * jax.nn.sigmoid(a) * b).astype(BF16)
    h1_ref[0:8, :] = h1_ref[tm:tm + 8, :]
    h2_ref[0:8, :] = h2_ref[tm:tm + 8, :]


def _ffn_up(x_bf, w_up, w_conv, layer):
    t = x_bf.shape[0]
    tm = min(1024, t)
    tn = 512
    nb = D_FF // tn
    wsp = lambda off: pl.BlockSpec((None, D_MODEL, tn), lambda j, i, off=off: (layer, 0, j + off))
    csp = lambda off: pl.BlockSpec((None, 3, tn), lambda j, i, off=off: (layer, 0, j + off))
    return pl.pallas_call(
        _ffn_up_kernel,
        grid=(nb, t // tm),
        in_specs=[pl.BlockSpec((tm, D_MODEL), lambda j, i: (i, 0)), wsp(0), wsp(nb), csp(0), csp(nb)],
        out_specs=pl.BlockSpec((tm, tn), lambda j, i: (i, j)),
        out_shape=jax.ShapeDtypeStruct((t, D_FF), BF16),
        scratch_shapes=[pltpu.VMEM((tm + 8, tn), F32), pltpu.VMEM((tm + 8, tn), F32),
                        pltpu.VMEM((D_MODEL, tn), BF16), pltpu.VMEM((D_MODEL, tn), BF16)],
        compiler_params=_params(("parallel", "arbitrary"), 48),
        name="ffn_up",
    )(x_bf, w_up, w_up, w_conv, w_conv)


def _ffn_down_kernel(a_ref, w_ref, x_ref, g_ref, b_ref, o_ref, obf_ref):
    _residual_ln_rows(a_ref, w_ref, x_ref, g_ref, b_ref, o_ref, obf_ref)


def _ffn_down(act, w_all, x, g, b, layer):
    t = x.shape[0]
    tm = min(256, t)
    row = pl.BlockSpec((tm, D_MODEL), lambda i: (i, 0))
    vec = pl.BlockSpec((None, 1, D_MODEL), lambda i: (layer, 0, 0))
    whole = pl.BlockSpec((None, D_FF, D_MODEL), lambda i: (layer, 0, 0), pipeline_mode=pl.Buffered(1))
    return pl.pallas_call(
        _ffn_down_kernel,
        grid=(t // tm,),
        in_specs=[pl.BlockSpec((tm, D_FF), lambda i: (i, 0)), whole, row, vec, vec],
        out_specs=[row, row],
        out_shape=[jax.ShapeDtypeStruct((t, D_MODEL), F32), jax.ShapeDtypeStruct((t, D_MODEL), BF16)],
        compiler_params=_params(("parallel",), 48),
        name="ffn_down",
    )(act, w_all, x, g, b)


def kernel(x, positions, w_in, ssm_lambda_re, ssm_lambda_im, ssm_log_step, ssm_b_re, ssm_b_im, ssm_c_re, ssm_c_im, ssm_d, ssm_glu, ret_norm_g, ret_norm_b, ret_out, rwkv_mu, rwkv_w0, rwkv_w2, rwkv_a0, rwkv_a2, rwkv_g2, rwkv_k_k, rwkv_k_a, rwkv_r_k, rwkv_norm_g, rwkv_norm_b, rwkv_out, w_o, ln1_g, ln1_b, ffn_up, ffn_conv, ffn_down, ln2_g, ln2_b):
    bsz, t, _ = x.shape
    assert bsz == 1
    depth = w_in.shape[0]
    vec3 = lambda p: p.reshape(depth, 1, p.shape[-1])
    down_bf = ffn_down.astype(BF16)

    twice = lambda a, b: jnp.concatenate([a, b], axis=-1)
    b_re_t, b_im_t = jnp.swapaxes(ssm_b_re, -1, -2), jnp.swapaxes(ssm_b_im, -1, -2)
    log_step = jnp.broadcast_to(ssm_log_step[..., None], ssm_lambda_re.shape)
    s5_prm = dict(
        lam_re=twice(ssm_lambda_re, ssm_lambda_re), lam_im=twice(ssm_lambda_im, ssm_lambda_im),
        log_step=twice(log_step, log_step),
        b_cat=twice(b_re_t, b_im_t), b_swp=twice(b_im_t, b_re_t),
        c_cat=twice(ssm_c_re, ssm_c_im), c_swp=twice(ssm_c_im, ssm_c_re),
        d=vec3(ssm_d))
    rwkv_prm = dict(mu=vec3(rwkv_mu), w0=vec3(rwkv_w0), a0=vec3(rwkv_a0), k_k=vec3(rwkv_k_k), k_a=vec3(rwkv_k_a),
                    r_k=vec3(rwkv_r_k), ng=vec3(rwkv_norm_g), nb=vec3(rwkv_norm_b),
                    w2=rwkv_w2.astype(BF16), a2=rwkv_a2.astype(BF16), g2=rwkv_g2.astype(BF16))
    ret_g, ret_b = vec3(ret_norm_g), vec3(ret_norm_b)
    ln1g, ln1b, ln2g, ln2b = vec3(ln1_g), vec3(ln1_b), vec3(ln2_g), vec3(ln2_b)

    cos_t, sin_t = _rope_tables(positions)
    xf = x.reshape(t, D_MODEL)
    xb = xf.astype(BF16)
    for l in range(depth):
        proj = _proj(xb, w_in, l, 0, OFF_GATE, 768)
        gates = _proj(xb, w_in, l, OFF_GATE, N_GATE, 768)
        z = _s5(proj, s5_prm, l).transpose(1, 0, 2).reshape(t, W_BR)
        o_ret = _retention(proj, cos_t, sin_t, ret_g, ret_b, l)
        o_rwkv = _rwkv(proj, rwkv_prm, l)
        merged = _merge(z, o_ret, o_rwkv, ssm_glu, ret_out, rwkv_out, gates, l)
        xf, xb = _out_ln(merged, w_o, xf, ln1g, ln1b, l)
        act = _ffn_up(xb, ffn_up, ffn_conv, l)
        xf, xb = _ffn_down(act, down_bf, xf, ln2g, ln2b, l)
    return xf.reshape(bsz, t, D_MODEL)
```

```python
import functools
import math

import jax
import jax.numpy as jnp
from jax import lax
from jax.experimental import pallas as pl
from jax.experimental.pallas import tpu as pltpu

F32 = jnp.float32
BF16 = jnp.bfloat16

D_MODEL = 2048
DEPTH = 4
W_BR = 1024
SSM_GROUP = 16
SSM_GROUPS = 64
SSM_STATE = 64
SSM_CHUNK = 16
SSM_GB = 8
RET_HEADS = 8
RET_DH = 128
RET_CHUNK = 128
RET_ROWS = 512
ROPE_BASE = 10000.0
RWKV_HEADS = 16
RWKV_DH = 64
RWKV_CHUNK = 64
RWKV_ROWS = 256
RWKV_IN = 3328
D_FF = 5632
OFF_RET = 1024
OFF_RWKV = 5120
OFF_GATE = 8448
N_GATE = 3 * D_MODEL
DEEPNORM_ALPHA = (2.0 * DEPTH) ** 0.25
LN_EPS = 1e-5
GN_EPS = 1e-5
RWKV_GN_EPS = 64e-5
MIB = 1024 * 1024
SUB_ROWS = 256


def _params(sem, vmem_mib):
    return pltpu.CompilerParams(dimension_semantics=sem, vmem_limit_bytes=vmem_mib * MIB)


def _dot(a, b):
    return jnp.dot(a, b, preferred_element_type=F32)


def _dot_nt(a, b):
    return lax.dot_general(a, b, (((1,), (1,)), ((), ())), preferred_element_type=F32)


def _dot_tn(a, b):
    return lax.dot_general(a, b, (((0,), (0,)), ((), ())), preferred_element_type=F32)


def _split2(x):
    hi = x.astype(BF16)
    lo = (x - hi.astype(F32)).astype(BF16)
    return hi, lo


def _split3(x):
    hi = x.astype(BF16)
    r1 = x - hi.astype(F32)
    mid = r1.astype(BF16)
    lo = (r1 - mid.astype(F32)).astype(BF16)
    return hi, mid, lo


def _layer_norm(y, g, b):
    mu = jnp.mean(y, axis=-1, keepdims=True)
    d = y - mu
    var = jnp.mean(d * d, axis=-1, keepdims=True)
    return d * lax.rsqrt(var + LN_EPS) * g + b


def _cast_once(w_ref, wbf_ref):
    @pl.when(pl.program_id(1) == 0)
    def _():
        wbf_ref[...] = w_ref[...].astype(BF16)


def _proj_kernel(x_ref, w_ref, o_ref, wbf_ref):
    _cast_once(w_ref, wbf_ref)
    o_ref[...] = _dot(x_ref[...], wbf_ref[...])


def _proj(x_bf, w_all, layer, col0, ncols, tn):
    t, k = x_bf.shape
    tm = min(2048, t)
    off = col0 // tn
    assert off * tn == col0 and ncols % tn == 0
    return pl.pallas_call(
        _proj_kernel,
        grid=(ncols // tn, t // tm),
        in_specs=[pl.BlockSpec((tm, k), lambda j, i: (i, 0)),
                  pl.BlockSpec((None, k, tn), lambda j, i: (layer, 0, j + off))],
        out_specs=pl.BlockSpec((tm, tn), lambda j, i: (i, j)),
        out_shape=jax.ShapeDtypeStruct((t, ncols), F32),
        scratch_shapes=[pltpu.VMEM((k, tn), BF16)],
        compiler_params=_params(("parallel", "arbitrary"), 56),
        name="proj",
    )(x_bf, w_all)


def _rope_kernel(pos_ref, cos_ref, sin_ref):
    half = RET_DH // 2
    pos = pos_ref[...].astype(F32)
    lane = lax.broadcasted_iota(jnp.int32, (1, RET_DH), 1)
    idx = jnp.where(lane < half, lane, lane - half).astype(F32)
    inv_freq = jnp.exp(-(idx / half) * math.log(ROPE_BASE))
    ang = pos * inv_freq
    cos_ref[...] = jnp.cos(ang)
    s = jnp.sin(ang)
    sin_ref[...] = jnp.where(lane < half, -s, s)


def _rope_tables(positions):
    t = positions.shape[-1]
    tm = min(1024, t)
    pos = positions.reshape(t, 1)
    return pl.pallas_call(
        _rope_kernel,
        grid=(t // tm,),
        in_specs=[pl.BlockSpec((tm, 1), lambda i: (i, 0))],
        out_specs=[pl.BlockSpec((tm, RET_DH), lambda i: (i, 0))] * 2,
        out_shape=[jax.ShapeDtypeStruct((t, RET_DH), F32)] * 2,
        compiler_params=_params(("parallel",), 16),
        name="rope",
    )(pos)


_RET_LOG_GAMMA = [math.log1p(-(2.0 ** (-5.0 - h))) for h in range(RET_HEADS)]


def _ret_kernel(q_ref, k_ref, v_ref, g_ref, cos_ref, sin_ref, ng_ref, nb_ref, o_ref, state_ref):
    c = RET_CHUNK

    @pl.when(pl.program_id(0) == 0)
    def _():
        state_ref[...] = jnp.zeros_like(state_ref)

    ri = lax.broadcasted_iota(jnp.int32, (c, c), 0).astype(F32)
    ci = lax.broadcasted_iota(jnp.int32, (c, c), 1).astype(F32)
    rel = ri - ci
    scale = RET_DH ** -0.5
    for h in range(RET_HEADS):
        sl = slice(h * RET_DH, (h + 1) * RET_DH)
        lg = _RET_LOG_GAMMA[h]
        decay = jnp.where(rel >= 0, jnp.exp(lg * jnp.maximum(rel, 0.0)), 0.0)
        q_decay = jnp.exp(lg * (ri + 1.0))
        k_decay = jnp.exp(lg * (c - 1.0 - ri))
        state = state_ref[h]
        for cc in range(q_ref.shape[0] // c):
            rows = slice(cc * c, (cc + 1) * c)
            cos = cos_ref[rows, :]
            sin = sin_ref[rows, :]
            q = q_ref[rows, sl]
            k = k_ref[rows, sl]
            qr = q * cos + pltpu.roll(q, RET_DH // 2, 1) * sin
            kr = (k * cos + pltpu.roll(k, RET_DH // 2, 1) * sin) * scale
            v_bf = v_ref[rows, sl].astype(BF16)
            q_bf = qr.astype(BF16)
            scores = _dot_nt(q_bf, kr.astype(BF16)) * decay
            inner = _dot(scores.astype(BF16), v_bf)
            cross = _dot(q_bf, state.astype(BF16)) * q_decay
            state = math.exp(lg * c) * state + _dot_tn((kr * k_decay).astype(BF16), v_bf)
            o = inner + cross
            mu = jnp.mean(o, axis=-1, keepdims=True)
            d = o - mu
            var = jnp.mean(d * d, axis=-1, keepdims=True)
            on = d * lax.rsqrt(var + GN_EPS) * ng_ref[:, sl] + nb_ref[:, sl]
            g = g_ref[rows, sl]
            o_ref[rows, sl] = (g * jax.nn.sigmoid(g) * on).astype(BF16)
        state_ref[h] = state


def _retention(proj, cos_t, sin_t, ng, nb, layer):
    ret = proj
    t = ret.shape[0]
    c = min(RET_ROWS, t)
    col = lambda j: pl.BlockSpec((c, W_BR), lambda i, j=j: (i, j + OFF_RET // W_BR))
    vec = pl.BlockSpec((None, 1, W_BR), lambda i: (layer, 0, 0))
    tab = pl.BlockSpec((c, RET_DH), lambda i: (i, 0))
    return pl.pallas_call(
        _ret_kernel,
        grid=(t // c,),
        in_specs=[col(0), col(1), col(2), col(3), tab, tab, vec, vec],
        out_specs=pl.BlockSpec((c, W_BR), lambda i: (i, 0)),
        out_shape=jax.ShapeDtypeStruct((t, W_BR), BF16),
        scratch_shapes=[pltpu.VMEM((RET_HEADS, RET_DH, RET_DH), F32)],
        compiler_params=_params(("arbitrary",), 32),
        name="retention",
    )(ret, ret, ret, ret, cos_t, sin_t, ng, nb)


def _seg64_sum(x):
    tile = 2 * RWKV_DH
    first = lax.broadcasted_iota(jnp.int32, (x.shape[0], tile), 1) < RWKV_DH
    outs = []
    for t in range(x.shape[1] // tile):
        xt = x[:, t * tile:(t + 1) * tile]
        s0 = jnp.sum(jnp.where(first, xt, 0.0), axis=-1, keepdims=True)
        s1 = jnp.sum(jnp.where(first, 0.0, xt), axis=-1, keepdims=True)
        outs.append(jnp.where(first, s0, s1))
    return jnp.concatenate(outs, axis=1)


def _cumsum_rows(x):
    rows = lax.broadcasted_iota(jnp.int32, x.shape, 0)
    s = 1
    while s < x.shape[0]:
        x = x + jnp.where(rows >= s, pltpu.roll(x, s, 0), 0.0)
        s *= 2
    return x


def _rwkv_kernel(r_ref, k_ref, v_ref, lo_ref, mu_ref, w0_ref, a0_ref, kk_ref, ka_ref, rk_ref, ng_ref, nb_ref,
                 w2_ref, a2_ref, g2_ref, o_ref, last_ref, s_ref):
    L = RWKV_CHUNK
    P = 2 * RWKV_DH
    nrows = r_ref.shape[0]

    @pl.when(pl.program_id(0) == 0)
    def _():
        last_ref[...] = jnp.zeros_like(last_ref)
        s_ref[...] = jnp.zeros_like(s_ref)

    def token_shift(z_ref, c0, c1):
        z = z_ref[...]
        rows = lax.broadcasted_iota(jnp.int32, z.shape, 0)
        zprev = jnp.where(rows == 0, last_ref[:, c0:c1], pltpu.roll(z, 1, 0))
        last_ref[:, c0:c1] = z[nrows - 1:, :]
        return z + mu_ref[:, c0:c1] * (zprev - z)

    r = token_shift(r_ref, 0, W_BR)
    k = token_shift(k_ref, W_BR, 2 * W_BR)
    v = token_shift(v_ref, 2 * W_BR, 3 * W_BR)
    lo = token_shift(lo_ref, 3 * W_BR, RWKV_IN)
    w_lo = lo[:, 0:64]
    a_lo = lo[:, 64:128]
    g_lo = lo[:, 128:256]

    wpre = w0_ref[...] + _dot(jnp.tanh(w_lo).astype(BF16), w2_ref[...])
    nw = -wpre
    softplus = jnp.maximum(nw, 0.0) + jnp.log1p(jnp.exp(-jnp.abs(nw)))
    w = -softplus - 0.5
    lw = -jnp.exp(w)
    a = jax.nn.sigmoid(a0_ref[...] + _dot(a_lo.astype(BF16), a2_ref[...]))
    g = _dot(jax.nn.sigmoid(g_lo).astype(BF16), g2_ref[...])

    kkv = k * kk_ref[...]
    kkn = kkv / jnp.maximum(jnp.sqrt(_seg64_sum(kkv * kkv)), 1e-12)
    k2 = k * (1.0 + (a - 1.0) * ka_ref[...])
    b = kkn * a

    chunks = range(nrows // L)
    rows_of = lambda x, c: x[c * L:(c + 1) * L, :]
    cum = jnp.concatenate([_cumsum_rows(rows_of(lw, c)) for c in chunks], axis=0)
    cum_last = jnp.concatenate(
        [jnp.broadcast_to(cum[(c + 1) * L - 1:(c + 1) * L, :], (L, W_BR)) for c in chunks], axis=0)
    e_to_end = jnp.exp(cum_last - cum)
    e_inv = jnp.exp(-cum)
    kkd = kkn * jnp.exp(cum - lw)
    rd = r * jnp.exp(cum)
    bd = b * e_inv
    kd = k2 * e_inv
    bd2 = b * e_to_end
    kd2 = k2 * e_to_end

    li = lax.broadcasted_iota(jnp.int32, (P, P), 0)
    lj = lax.broadcasted_iota(jnp.int32, (P, P), 1)
    tt = li % L
    ts = lj % L
    strict = tt > ts
    incl = tt >= ts
    eye = jnp.where(li == lj, 1.0, 0.0)
    first_half = lax.broadcasted_iota(jnp.int32, (L, P), 1) < RWKV_DH

    def stack(x, c, p):
        xp = x[c * L:(c + 1) * L, p * P:(p + 1) * P]
        return jnp.concatenate([jnp.where(first_half, xp, 0.0), jnp.where(first_half, 0.0, xp)],
                               axis=0).astype(BF16)

    pairs = range(RWKV_HEADS // 2)
    cps = [(c, p) for c in chunks for p in pairs]
    kkm = {cp: stack(kkd, *cp) for cp in cps}
    rm = {cp: stack(rd, *cp) for cp in cps}
    vm = {cp: stack(v, *cp) for cp in cps}
    gram = {cp: _dot_nt(jnp.concatenate([kkm[cp], rm[cp]], axis=0),
                        jnp.concatenate([stack(bd, *cp), stack(kd, *cp)], axis=0)) for cp in cps}
    a_bb = {cp: jnp.where(strict, gram[cp][:P, :P], 0.0) for cp in cps}
    av = {cp: _dot(jnp.where(strict, gram[cp][:P, P:], 0.0).astype(BF16), vm[cp]) for cp in cps}
    r_bk = {cp: jnp.concatenate([jnp.where(incl, gram[cp][P:, :P], 0.0), jnp.where(incl, gram[cp][P:, P:], 0.0)],
                                axis=1).astype(BF16) for cp in cps}
    lvl = (tt // 2) == (ts // 2)
    t_inv = {cp: eye - jnp.where(lvl, a_bb[cp], 0.0) for cp in cps}
    blk = 2
    while blk < L:
        lvl = ((tt // (2 * blk)) == (ts // (2 * blk))) & ((tt // blk) != (ts // blk))
        t_bf = {cp: t_inv[cp].astype(BF16) for cp in cps}
        mt = {cp: _dot(jnp.where(lvl, a_bb[cp], 0.0).astype(BF16), t_bf[cp]).astype(BF16) for cp in cps}
        t_inv = {cp: t_inv[cp] - _dot(t_bf[cp], mt[cp]) for cp in cps}
        blk *= 2
    t_bf = {cp: t_inv[cp].astype(BF16) for cp in cps}

    state = [s_ref[p] for p in pairs]
    y_rows = []
    for c in chunks:
        s_bf = [state[p].astype(BF16) for p in pairs]
        wm = [(_dot_nt(kkm[c, p], s_bf[p]) + av[c, p]).astype(BF16) for p in pairs]
        ys0 = [_dot_nt(rm[c, p], s_bf[p]) for p in pairs]
        bkm2 = [jnp.concatenate([stack(bd2, c, p), stack(kd2, c, p)], axis=0) for p in pairs]
        p_last = jnp.exp(cum[(c + 1) * L - 1:(c + 1) * L, :])
        uv = [jnp.concatenate([(-_dot(t_bf[c, p], wm[p])).astype(BF16), vm[c, p]], axis=0)
              for p in pairs]
        state = [state[p] * p_last[:, p * P:(p + 1) * P] + _dot_tn(uv[p], bkm2[p]) for p in pairs]
        ys = [ys0[p] + _dot(r_bk[c, p], uv[p]) for p in pairs]
        y_rows.append(jnp.concatenate([y[:L] + y[L:] for y in ys], axis=1))
    for p in pairs:
        s_ref[p] = state[p]
    y = jnp.concatenate(y_rows, axis=0)

    inv_n = 1.0 / RWKV_DH
    mu = _seg64_sum(y) * inv_n
    d = y - mu
    var = _seg64_sum(d * d) * inv_n
    yn = d * lax.rsqrt(var + RWKV_GN_EPS) * ng_ref[...] + nb_ref[...]
    bonus = _seg64_sum(r * k2 * rk_ref[...]) * v
    o_ref[...] = ((yn + bonus) * g).astype(BF16)


def _rwkv(proj, prm, layer):
    t = proj.shape[0]
    tr = min(RWKV_ROWS, t)
    lo_w = RWKV_IN - 3 * W_BR
    vec = lambda n: pl.BlockSpec((None, 1, n), lambda i: (layer, 0, 0))
    mat = lambda r: pl.BlockSpec((None, r, W_BR), lambda i: (layer, 0, 0))
    wide = lambda j: pl.BlockSpec((tr, W_BR), lambda i, j=j: (i, OFF_RWKV // W_BR + j))
    return pl.pallas_call(
        _rwkv_kernel,
        grid=(t // tr,),
        in_specs=[wide(0), wide(1), wide(2),
                  pl.BlockSpec((tr, lo_w), lambda i: (i, (OFF_RWKV + 3 * W_BR) // lo_w)), vec(RWKV_IN)]
                 + [vec(W_BR)] * 7 + [mat(64), mat(64), mat(128)],
        out_specs=pl.BlockSpec((tr, W_BR), lambda i: (i, 0)),
        out_shape=jax.ShapeDtypeStruct((t, W_BR), BF16),
        scratch_shapes=[pltpu.VMEM((1, RWKV_IN), F32),
                        pltpu.VMEM((RWKV_HEADS // 2, 2 * RWKV_DH, 2 * RWKV_DH), F32)],
        compiler_params=_params(("arbitrary",), 32),
        name="rwkv7",
    )(proj, proj, proj, proj, prm["mu"], prm["w0"], prm["a0"], prm["k_k"], prm["k_a"], prm["r_k"], prm["ng"],
      prm["nb"], prm["w2"], prm["a2"], prm["g2"])


def _gelu_tanh(x):
    c = math.sqrt(2.0 / math.pi)
    return x * (0.5 * (1.0 + jnp.tanh(c * (x + 0.044715 * (x * x * x)))))


def _s5_kernel(u_ref, lre_ref, lim_ref, ls_ref, bcat_ref, bswp_ref, ccat_ref, cswp_ref, d_ref, z_ref,
               et_ref, bb_ref, fbig_ref, wot_ref, mbig_ref, x_ref, prev_ref, a_ref):
    C = SSM_CHUNK
    GL = SSM_GROUP * SSM_GB
    PS = SSM_STATE
    half = SSM_GB * PS
    n = u_ref.shape[0] // C
    u_step = lambda j: u_ref[pl.ds(j, n, stride=C), :]

    @pl.when(pl.program_id(0) == 0)
    def _():
        fbig_ref[...] = jnp.zeros_like(fbig_ref)
        wot_ref[...] = jnp.zeros_like(wot_ref)
        mbig_ref[...] = jnp.zeros_like(mbig_ref)

    first1 = lax.broadcasted_iota(jnp.int32, (1, GL), 1) < PS
    sgn = jnp.where(first1, -1.0, 1.0)
    first16 = lax.broadcasted_iota(jnp.int32, (SSM_GROUP, GL), 1) < PS
    tau = lax.broadcasted_iota(jnp.int32, (24, GL), 0).astype(F32)

    for g in range(SSM_GB):
        lr = lre_ref[g:g + 1, :]
        li = lim_ref[g:g + 1, :]
        dt = jnp.exp(ls_ref[g:g + 1, :])
        mag = jnp.exp(lr * dt)
        ab_re = mag * jnp.cos(li * dt)
        ab_im = mag * jnp.sin(li * dt)
        denom = lr * lr + li * li
        f_re = ((ab_re - 1.0) * lr + ab_im * li) / denom
        f_im = (ab_im * lr - (ab_re - 1.0) * li) / denom
        bcat = bcat_ref[g]
        bswp = bswp_ref[g]
        bbs = f_re * bcat + (sgn * f_im) * bswp
        bbw = f_re * bswp - (sgn * f_im) * bcat
        pmag = jnp.exp(tau * (lr * dt))
        pw_re = pmag * jnp.cos(tau * (li * dt))
        pw_im = pmag * jnp.sin(tau * (li * dt))
        ccat = ccat_ref[g]
        cswp = cswp_ref[g]
        rows = slice(g * SSM_GROUP, (g + 1) * SSM_GROUP)
        tile = slice((g // 2) * GL, (g // 2 + 1) * GL)
        tile_im = slice(half + (g // 2) * GL, half + (g // 2 + 1) * GL)
        odd = g % 2

        def planes(xy, yx):
            if odd:
                return jnp.where(first16, 0.0, yx).astype(BF16), jnp.where(first16, 0.0, xy).astype(BF16)
            return jnp.where(first16, xy, 0.0).astype(BF16), jnp.where(first16, yx, 0.0).astype(BF16)

        for t in range(C + 1):
            pr = pw_re[t:t + 1, :]
            pi = pw_im[t:t + 1, :]
            e = pr * (-sgn * ccat) - pi * cswp
            et_ref[t, rows, :] = e
            if t >= 1:
                esw = pr * (sgn * cswp) - pi * ccat
                o_re, o_im = planes(e, esw)
                orow = slice((t - 1) * GL + g * SSM_GROUP, (t - 1) * GL + (g + 1) * SSM_GROUP)
                wot_ref[orow, tile] = o_re
                wot_ref[orow, tile_im] = o_im
            if t < C:
                qr = pw_re[C - 1 - t:C - t, :]
                qi = pw_im[C - 1 - t:C - t, :]
                f = qr * bbs + qi * (sgn * bbw)
                fsw = qr * bbw - qi * (sgn * bbs)
                i_re, i_im = planes(f, fsw)
                irow = slice(t * GL + g * SSM_GROUP, t * GL + (g + 1) * SSM_GROUP)
                fbig_ref[irow, tile] = i_re
                fbig_ref[irow, tile_im] = i_im
        bb_ref[rows, :] = bbs
        lanes = slice(odd * PS, (odd + 1) * PS)
        dst = slice((g // 2) * GL + odd * PS, (g // 2) * GL + (odd + 1) * PS)
        a_ref[0:1, dst] = pw_re[C:C + 1, lanes]
        a_ref[1:2, dst] = pw_im[C:C + 1, lanes]

    bi = lax.broadcasted_iota(jnp.int32, (GL, GL), 0) // SSM_GROUP
    bj = lax.broadcasted_iota(jnp.int32, (GL, GL), 1) // SSM_GROUP
    bh, bl = _split2(bb_ref[...])
    for t in range(C):
        eh, el = _split2(et_ref[t])
        kt = jnp.where(bi == bj, _dot_nt(bh, eh) + _dot_nt(bh, el) + _dot_nt(bl, eh), 0.0).astype(BF16)
        for j in range(C - t):
            mbig_ref[j * GL:(j + 1) * GL, (j + t) * GL:(j + t + 1) * GL] = kt

    ucat = jnp.concatenate([u_step(j).astype(BF16) for j in range(C)], axis=1)
    x_ref[...] = _dot(ucat, fbig_ref[...])

    a_re = a_ref[0:1, :]
    a_im = a_ref[1:2, :]

    def step(i, carry):
        s_re, s_im = carry
        prev_ref[pl.ds(i, 1), 0:half] = s_re
        prev_ref[pl.ds(i, 1), half:2 * half] = s_im
        x_re = x_ref[pl.ds(i, 1), 0:half]
        x_im = x_ref[pl.ds(i, 1), half:2 * half]
        return (a_re * s_re - a_im * s_im + x_re, a_re * s_im + a_im * s_re + x_im)

    zero = jnp.zeros((1, half), F32)
    lax.fori_loop(0, n, step, (zero, zero), unroll=8)

    prev_bf = prev_ref[...].astype(BF16)
    steps_per_dot = 4
    for q in range(C // steps_per_dot):
        kq = (q + 1) * steps_per_dot * GL
        cols = slice(q * steps_per_dot * GL, (q + 1) * steps_per_dot * GL)
        y = _dot(ucat[:, :kq], mbig_ref[0:kq, cols]) + _dot_nt(prev_bf, wot_ref[cols, :])
        for ii in range(steps_per_dot):
            i = q * steps_per_dot + ii
            yi = y[:, ii * GL:(ii + 1) * GL] + d_ref[...] * u_step(i)
            z_ref[i] = _gelu_tanh(yi).astype(BF16)


def _s5(proj, prm, layer):
    t = proj.shape[0]
    C = SSM_CHUNK
    n = t // C
    gl = SSM_GROUP * SSM_GB
    nblk = W_BR // gl
    st = 2 * SSM_GB * SSM_STATE
    lam = pl.BlockSpec((None, SSM_GB, gl), lambda b: (layer, b, 0))
    bc = pl.BlockSpec((None, SSM_GB, SSM_GROUP, gl), lambda b: (layer, b, 0, 0))
    return pl.pallas_call(
        _s5_kernel,
        grid=(nblk,),
        in_specs=[pl.BlockSpec((t, gl), lambda b: (0, b)), lam, lam, lam, bc, bc, bc, bc,
                  pl.BlockSpec((None, 1, gl), lambda b: (layer, 0, b))],
        out_specs=pl.BlockSpec((C, n, gl), lambda b: (0, 0, b)),
        out_shape=jax.ShapeDtypeStruct((C, n, W_BR), BF16),
        scratch_shapes=[pltpu.VMEM((C + 1, gl, gl), F32), pltpu.VMEM((gl, gl), F32),
                        pltpu.VMEM((C * gl, st), BF16), pltpu.VMEM((C * gl, st), BF16),
                        pltpu.VMEM((C * gl, C * gl), BF16),
                        pltpu.VMEM((n, st), F32), pltpu.VMEM((n, st), F32), pltpu.VMEM((8, st // 2), F32)],
        compiler_params=_params(("arbitrary",), 56),
        name="s5",
    )(proj, prm["lam_re"], prm["lam_im"], prm["log_step"], prm["b_cat"], prm["b_swp"], prm["c_cat"],
      prm["c_swp"], prm["d"])


def _merge_kernel(z_ref, r_ref, k_ref, wa_ref, wb_ref, wr_ref, wk_ref, g0_ref, g1_ref, g2_ref, o_ref,
                  wa_bf, wb_bf, wr_bf, wk_bf):
    for w_ref, w_bf in ((wa_ref, wa_bf), (wb_ref, wb_bf), (wr_ref, wr_bf), (wk_ref, wk_bf)):
        _cast_once(w_ref, w_bf)
    sub = min(SUB_ROWS, z_ref.shape[0])
    for s in range(z_ref.shape[0] // sub):
        rows = slice(s * sub, (s + 1) * sub)
        z = z_ref[rows, :]
        y_ssm = _dot(z, wa_bf[...]) * jax.nn.sigmoid(_dot(z, wb_bf[...]))
        y_ret = _dot(r_ref[rows, :], wr_bf[...])
        y_rwkv = _dot(k_ref[rows, :], wk_bf[...])
        gate = lambda g_ref: jax.nn.sigmoid(g_ref[rows, :])
        o_ref[rows, :] = (gate(g0_ref) * y_ssm + gate(g1_ref) * y_ret + gate(g2_ref) * y_rwkv).astype(BF16)


def _merge(z, o_ret, o_rwkv, w_glu, w_ret, w_rwkv, gates, layer):
    t = z.shape[0]
    tm = min(512, t)
    tn = 512
    nb = D_MODEL // tn
    act = pl.BlockSpec((tm, W_BR), lambda j, i: (i, 0))
    wsp = lambda off: pl.BlockSpec((None, W_BR, tn), lambda j, i, off=off: (layer, 0, j + off))
    gsp = lambda off: pl.BlockSpec((tm, tn), lambda j, i, off=off: (i, j + off))
    return pl.pallas_call(
        _merge_kernel,
        grid=(nb, t // tm),
        in_specs=[act, act, act, wsp(0), wsp(nb), wsp(0), wsp(0), gsp(0), gsp(nb), gsp(2 * nb)],
        out_specs=pl.BlockSpec((tm, tn), lambda j, i: (i, j)),
        out_shape=jax.ShapeDtypeStruct((t, D_MODEL), BF16),
        scratch_shapes=[pltpu.VMEM((W_BR, tn), BF16)] * 4,
        compiler_params=_params(("parallel", "arbitrary"), 48),
        name="merge",
    )(z, o_ret, o_rwkv, w_glu, w_glu, w_ret, w_rwkv, gates, gates, gates)


def _residual_ln_rows(a_ref, w_bf, x_ref, g_ref, b_ref, o_ref, obf_ref):
    sub = min(128, a_ref.shape[0])
    for s in range(a_ref.shape[0] // sub):
        rows = slice(s * sub, (s + 1) * sub)
        y = DEEPNORM_ALPHA * x_ref[rows, :] + _dot(a_ref[rows, :], w_bf[...])
        out = _layer_norm(y, g_ref[...], b_ref[...])
        o_ref[rows, :] = out
        obf_ref[rows, :] = out.astype(BF16)


def _out_ln_kernel(m_ref, w_ref, x_ref, g_ref, b_ref, o_ref, obf_ref, w_bf):
    @pl.when(pl.program_id(0) == 0)
    def _():
        w_bf[...] = w_ref[...].astype(BF16)

    _residual_ln_rows(m_ref, w_bf, x_ref, g_ref, b_ref, o_ref, obf_ref)


def _out_ln(m, w_all, x, g, b, layer):
    t = x.shape[0]
    tm = min(256, t)
    row = pl.BlockSpec((tm, D_MODEL), lambda i: (i, 0))
    vec = pl.BlockSpec((None, 1, D_MODEL), lambda i: (layer, 0, 0))
    whole = pl.BlockSpec((None, D_MODEL, D_MODEL), lambda i: (layer, 0, 0), pipeline_mode=pl.Buffered(1))
    return pl.pallas_call(
        _out_ln_kernel,
        grid=(t // tm,),
        in_specs=[row, whole, row, vec, vec],
        out_specs=[row, row],
        out_shape=[jax.ShapeDtypeStruct((t, D_MODEL), F32), jax.ShapeDtypeStruct((t, D_MODEL), BF16)],
        scratch_shapes=[pltpu.VMEM((D_MODEL, D_MODEL), BF16)],
        compiler_params=_params(("arbitrary",), 48),
        name="out_ln",
    )(m, w_all, x, g, b)


def _ffn_up_kernel(x_ref, w1_ref, w2_ref, c1_ref, c2_ref, o_ref, h1_ref, h2_ref, w1_bf, w2_bf):
    tm = x_ref.shape[0]
    _cast_once(w1_ref, w1_bf)
    _cast_once(w2_ref, w2_bf)

    @pl.when(pl.program_id(1) == 0)
    def _():
        h1_ref[0:8, :] = jnp.zeros((8, h1_ref.shape[1]), F32)
        h2_ref[0:8, :] = jnp.zeros((8, h2_ref.shape[1]), F32)

    sub = tm
    for s in range(tm // sub):
        r0 = s * sub
        x = x_ref[r0:r0 + sub, :]

        def conv(w_bf, c_ref, h_ref):
            h = _dot(x, w_bf[...])
            h_ref[8 + r0:8 + r0 + sub, :] = h
            return (c_ref[2:3, :] * h + c_ref[1:2, :] * h_ref[7 + r0:7 + r0 + sub, :]
                    + c_ref[0:1, :] * h_ref[6 + r0:6 + r0 + sub, :])

        a = conv(w1_bf, c1_ref, h1_ref)
        b = conv(w2_bf, c2_ref, h2_ref)
        o_ref[r0:r0 + sub, :] = (a * jax.nn.sigmoid(a) * b).astype(BF16)
    h1_ref[0:8, :] = h1_ref[tm:tm + 8, :]
    h2_ref[0:8, :] = h2_ref[tm:tm + 8, :]


def _ffn_up(x_bf, w_up, w_conv, layer):
    t = x_bf.shape[0]
    tm = min(1024, t)
    tn = 512
    nb = D_FF // tn
    wsp = lambda off: pl.BlockSpec((None, D_MODEL, tn), lambda j, i, off=off: (layer, 0, j + off))
    csp = lambda off: pl.BlockSpec((None, 3, tn), lambda j, i, off=off: (layer, 0, j + off))
    return pl.pallas_call(
        _ffn_up_kernel,
        grid=(nb, t // tm),
        in_specs=[pl.BlockSpec((tm, D_MODEL), lambda j, i: (i, 0)), wsp(0), wsp(nb), csp(0), csp(nb)],
        out_specs=pl.BlockSpec((tm, tn), lambda j, i: (i, j)),
        out_shape=jax.ShapeDtypeStruct((t, D_FF), BF16),
        scratch_shapes=[pltpu.VMEM((tm + 8, tn), F32), pltpu.VMEM((tm + 8, tn), F32),
                        pltpu.VMEM((D_MODEL, tn), BF16), pltpu.VMEM((D_MODEL, tn), BF16)],
        compiler_params=_params(("parallel", "arbitrary"), 48),
        name="ffn_up",
    )(x_bf, w_up, w_up, w_conv, w_conv)


def _ffn_down_kernel(a_ref, w_ref, x_ref, g_ref, b_ref, o_ref, obf_ref):
    _residual_ln_rows(a_ref, w_ref, x_ref, g_ref, b_ref, o_ref, obf_ref)


def _ffn_down(act, w_all, x, g, b, layer):
    t = x.shape[0]
    tm = min(256, t)
    row = pl.BlockSpec((tm, D_MODEL), lambda i: (i, 0))
    vec = pl.BlockSpec((None, 1, D_MODEL), lambda i: (layer, 0, 0))
    whole = pl.BlockSpec((None, D_FF, D_MODEL), lambda i: (layer, 0, 0), pipeline_mode=pl.Buffered(1))
    return pl.pallas_call(
        _ffn_down_kernel,
        grid=(t // tm,),
        in_specs=[pl.BlockSpec((tm, D_FF), lambda i: (i, 0)), whole, row, vec, vec],
        out_specs=[row, row],
        out_shape=[jax.ShapeDtypeStruct((t, D_MODEL), F32), jax.ShapeDtypeStruct((t, D_MODEL), BF16)],
        compiler_params=_params(("parallel",), 48),
        name="ffn_down",
    )(act, w_all, x, g, b)


def kernel(x, positions, w_in, ssm_lambda_re, ssm_lambda_im, ssm_log_step, ssm_b_re, ssm_b_im, ssm_c_re, ssm_c_im, ssm_d, ssm_glu, ret_norm_g, ret_norm_b, ret_out, rwkv_mu, rwkv_w0, rwkv_w2, rwkv_a0, rwkv_a2, rwkv_g2, rwkv_k_k, rwkv_k_a, rwkv_r_k, rwkv_norm_g, rwkv_norm_b, rwkv_out, w_o, ln1_g, ln1_b, ffn_up, ffn_conv, ffn_down, ln2_g, ln2_b):
    bsz, t, _ = x.shape
    assert bsz == 1
    depth = w_in.shape[0]
    vec3 = lambda p: p.reshape(depth, 1, p.shape[-1])
    down_bf = ffn_down.astype(BF16)

    twice = lambda a, b: jnp.concatenate([a, b], axis=-1)
    b_re_t, b_im_t = jnp.swapaxes(ssm_b_re, -1, -2), jnp.swapaxes(ssm_b_im, -1, -2)
    log_step = jnp.broadcast_to(ssm_log_step[..., None], ssm_lambda_re.shape)
    s5_prm = dict(
        lam_re=twice(ssm_lambda_re, ssm_lambda_re), lam_im=twice(ssm_lambda_im, ssm_lambda_im),
        log_step=twice(log_step, log_step),
        b_cat=twice(b_re_t, b_im_t), b_swp=twice(b_im_t, b_re_t),
        c_cat=twice(ssm_c_re, ssm_c_im), c_swp=twice(ssm_c_im, ssm_c_re),
        d=vec3(ssm_d))
    rwkv_prm = dict(mu=vec3(rwkv_mu), w0=vec3(rwkv_w0), a0=vec3(rwkv_a0), k_k=vec3(rwkv_k_k), k_a=vec3(rwkv_k_a),
                    r_k=vec3(rwkv_r_k), ng=vec3(rwkv_norm_g), nb=vec3(rwkv_norm_b),
                    w2=rwkv_w2.astype(BF16), a2=rwkv_a2.astype(BF16), g2=rwkv_g2.astype(BF16))
    ret_g, ret_b = vec3(ret_norm_g), vec3(ret_norm_b)
    ln1g, ln1b, ln2g, ln2b = vec3(ln1_g), vec3(ln1_b), vec3(ln2_g), vec3(ln2_b)

    cos_t, sin_t = _rope_tables(positions)
    xf = x.reshape(t, D_MODEL)
    xb = xf.astype(BF16)
    for l in range(depth):
        proj = _proj(xb, w_in, l, 0, OFF_GATE, 768)
        gates = _proj(xb, w_in, l, OFF_GATE, N_GATE, 768)
        z = _s5(proj, s5_prm, l).transpose(1, 0, 2).reshape(t, W_BR)
        o_ret = _retention(proj, cos_t, sin_t, ret_g, ret_b, l)
        o_rwkv = _rwkv(proj, rwkv_prm, l)
        merged = _merge(z, o_ret, o_rwkv, ssm_glu, ret_out, rwkv_out, gates, l)
        xf, xb = _out_ln(merged, w_o, xf, ln1g, ln1b, l)
        act = _ffn_up(xb, ffn_up, ffn_conv, l)
        xf, xb = _ffn_down(act, down_bf, xf, ln2g, ln2b, l)
    return xf.reshape(bsz, t, D_MODEL)
```
